```python
import math
import jax, jax.numpy as jnp
from jax import lax
import numpy as np

D_MODEL = 1024
BATCH = 8
SEQ = 4096
DEPTH = 2

N_MIXERS = 2
EPS = 1e-6
BLOCK = 128
SWA_HEADS = 16
SWA_KV_HEADS = 2
SWA_HEAD_DIM = 64
WINDOW = 128
SWA_GROUP = SWA_HEADS // SWA_KV_HEADS
SWA_QKV_DIM = (SWA_HEADS + 2 * SWA_KV_HEADS) * SWA_HEAD_DIM
DIFF_HEADS = 8
DIFF_HEAD_DIM = 64
DIFF_QKV_DIM = 3 * DIFF_HEADS * 2 * DIFF_HEAD_DIM
D_FF = 2816
CONV_WIDTH = 3

N_A_LAYERS = (DEPTH + 1) // 2
N_B_LAYERS = DEPTH // 2

kernel_name = "hybrid_swa_sink_diffattn_alibi_convffn"


def rmsnorm(x, g):
    xf = x.astype(jnp.float32)
    y = xf * lax.rsqrt(jnp.mean(xf * xf, axis=-1, keepdims=True) + EPS)
    return (y * g.astype(jnp.float32)).astype(x.dtype)


def alibi_slopes(n_heads):
    return jnp.exp2(-8.0 * (jnp.arange(n_heads, dtype=jnp.float32) + 1.0) / n_heads)


def diff_lambda_init(layer_idx):
    return 0.8 - 0.6 * math.exp(-0.3 * layer_idx)


def sliding_window_attention(h, w_qkv, sinks, w_o):
    B, S, _ = h.shape
    nb = S // BLOCK
    qkv = h @ w_qkv
    q, k, v = jnp.split(qkv, [SWA_HEADS * SWA_HEAD_DIM,
                              (SWA_HEADS + SWA_KV_HEADS) * SWA_HEAD_DIM], axis=-1)
    q = q.reshape(B, nb, BLOCK, SWA_KV_HEADS, SWA_GROUP, SWA_HEAD_DIM)
    k = k.reshape(B, nb, BLOCK, SWA_KV_HEADS, SWA_HEAD_DIM)
    v = v.reshape(B, nb, BLOCK, SWA_KV_HEADS, SWA_HEAD_DIM)

    def with_prev(t):
        prev = jnp.pad(t[:, :-1], ((0, 0), (1, 0), (0, 0), (0, 0), (0, 0)))
        return jnp.concatenate([prev, t], axis=2)

    kb, vb = with_prev(k), with_prev(v)
    scale = SWA_HEAD_DIM ** -0.5
    scores = jnp.einsum('bnqkgd,bnskd->bnkgqs', q, kb).astype(jnp.float32) * scale
    qi = jnp.arange(BLOCK)[:, None]
    kj = jnp.arange(2 * BLOCK)[None, :]
    dist = qi + BLOCK - kj
    kpos = jnp.arange(nb)[:, None, None] * BLOCK - BLOCK + kj[None]
    valid = (dist >= 0)[None] & (dist < WINDOW)[None] & (kpos >= 0)
    slopes = alibi_slopes(SWA_HEADS).reshape(SWA_KV_HEADS, SWA_GROUP)
    bias = -slopes[:, :, None, None] * dist.astype(jnp.float32)
    scores = jnp.where(valid[None, :, None, None], scores + bias[None, None], -jnp.inf)
    sink = sinks.astype(jnp.float32).reshape(SWA_KV_HEADS, SWA_GROUP)[None, None, :, :, None, None]
    sink = jnp.broadcast_to(sink, scores.shape[:-1] + (1,))
    probs = jax.nn.softmax(jnp.concatenate([scores, sink], axis=-1), axis=-1)[..., :-1]
    out = jnp.einsum('bnkgqs,bnskd->bnqkgd', probs.astype(vb.dtype), vb)
    return out.reshape(B, S, SWA_HEADS * SWA_HEAD_DIM) @ w_o


def differential_attention(h, w_qkv, lam_q1, lam_k1, lam_q2, lam_k2, subln_g, w_o, lambda_init):
    B, S, _ = h.shape
    H, d = DIFF_HEADS, DIFF_HEAD_DIM
    nqb = S // BLOCK
    qkv = h @ w_qkv
    q, k, v = jnp.split(qkv, 3, axis=-1)
    q = q.reshape(B, S, H, 2, d)
    k = k.reshape(B, S, H, 2, d)
    v = v.reshape(B, S, H, 2 * d)
    lam = (jnp.exp(jnp.sum(lam_q1.astype(jnp.float32) * lam_k1.astype(jnp.float32)))
           - jnp.exp(jnp.sum(lam_q2.astype(jnp.float32) * lam_k2.astype(jnp.float32)))
           + lambda_init)
    slopes = alibi_slopes(H)[:, None, None, None]
    scale = d ** -0.5
    kpos = jnp.arange(S)
    qb = jnp.moveaxis(q.reshape(B, nqb, BLOCK, H, 2, d), 1, 0)

    def query_block(args):
        q_blk, n = args
        s = jnp.einsum('bqhcd,bshcd->bhcqs', q_blk, k).astype(jnp.float32) * scale
        qpos = n * BLOCK + jnp.arange(BLOCK)
        dist = (qpos[:, None] - kpos[None, :])
        s = jnp.where(dist >= 0, s - slopes * dist.astype(jnp.float32), -jnp.inf)
        p = jax.nn.softmax(s, axis=-1)
        a = p[:, :, 0] - lam * p[:, :, 1]
        return jnp.einsum('bhqs,bshe->bqhe', a.astype(v.dtype), v)

    o = lax.map(query_block, (qb, jnp.arange(nqb)))
    o = jnp.moveaxis(o, 0, 1).reshape(B, S, H, 2 * d)
    o = rmsnorm(o, subln_g) * (1.0 - lambda_init)
    return o.reshape(B, S, H * 2 * d) @ w_o


def conv_ffn(h, w_up, conv_w, conv_b, w_down):
    S = h.shape[1]
    u = h @ w_up
    up = jnp.pad(u, ((0, 0), (CONV_WIDTH - 1, 0), (0, 0)))
    c = conv_b
    for tap in range(CONV_WIDTH):
        c = c + conv_w[tap] * up[:, tap:tap + S]
    g, val = jnp.split(c, 2, axis=-1)
    return (jax.nn.gelu(g, approximate=True) * val) @ w_down


def setup_inputs(seed: int = 0) -> dict:
    key = jax.random.key(seed)
    ks = jax.random.split(key, 24)
    f32 = jnp.float32

    def nrm(k, shape, scale):
        return jax.random.normal(k, shape, f32) * scale

    def gain(k, shape):
        return 1.0 + 0.02 * jax.random.normal(k, shape, f32)

    return {
        "x": jax.random.normal(ks[0], (BATCH, SEQ, D_MODEL), f32),
        "mix_pre_g": gain(ks[1], (DEPTH, D_MODEL)),
        "mix_post_g": gain(ks[2], (DEPTH, D_MODEL)),
        "ffn_pre_g": gain(ks[3], (DEPTH, D_MODEL)),
        "ffn_post_g": gain(ks[4], (DEPTH, D_MODEL)),
        "swa_w_qkv": nrm(ks[5], (N_A_LAYERS, D_MODEL, SWA_QKV_DIM), D_MODEL ** -0.5),
        "swa_sinks": nrm(ks[6], (N_A_LAYERS, SWA_HEADS), 1.0),
        "swa_w_o": nrm(ks[7], (N_A_LAYERS, SWA_HEADS * SWA_HEAD_DIM, D_MODEL),
                        (SWA_HEADS * SWA_HEAD_DIM) ** -0.5),
        "diff_w_qkv": nrm(ks[8], (N_B_LAYERS, D_MODEL, DIFF_QKV_DIM), D_MODEL ** -0.5),
        "diff_lam_q1": nrm(ks[9], (N_B_LAYERS, DIFF_HEAD_DIM), 0.1),
        "diff_lam_k1": nrm(ks[10], (N_B_LAYERS, DIFF_HEAD_DIM), 0.1),
        "diff_lam_q2": nrm(ks[11], (N_B_LAYERS, DIFF_HEAD_DIM), 0.1),
        "diff_lam_k2": nrm(ks[12], (N_B_LAYERS, DIFF_HEAD_DIM), 0.1),
        "diff_subln_g": gain(ks[13], (N_B_LAYERS, 2 * DIFF_HEAD_DIM)),
        "diff_w_o": nrm(ks[14], (N_B_LAYERS, DIFF_HEADS * 2 * DIFF_HEAD_DIM, D_MODEL),
                         (DIFF_HEADS * 2 * DIFF_HEAD_DIM) ** -0.5),
        "ffn_w_up": nrm(ks[15], (DEPTH, D_MODEL, 2 * D_FF), D_MODEL ** -0.5),
        "ffn_conv_w": nrm(ks[16], (DEPTH, CONV_WIDTH, 2 * D_FF), CONV_WIDTH ** -0.5),
        "ffn_conv_b": nrm(ks[17], (DEPTH, 2 * D_FF), 0.01),
        "ffn_w_down": nrm(ks[18], (DEPTH, D_FF, D_MODEL), D_FF ** -0.5),
    }


def reference(x, mix_pre_g, mix_post_g, ffn_pre_g, ffn_post_g,
              swa_w_qkv, swa_sinks, swa_w_o,
              diff_w_qkv, diff_lam_q1, diff_lam_k1, diff_lam_q2, diff_lam_k2,
              diff_subln_g, diff_w_o,
              ffn_w_up, ffn_conv_w, ffn_conv_b, ffn_w_down):
    for i in range(DEPTH):
        h = rmsnorm(x, mix_pre_g[i])
        j = i // N_MIXERS
        if i % N_MIXERS == 0:
            m = sliding_window_attention(h, swa_w_qkv[j], swa_sinks[j], swa_w_o[j])
        else:
            m = differential_attention(h, diff_w_qkv[j], diff_lam_q1[j], diff_lam_k1[j],
                                       diff_lam_q2[j], diff_lam_k2[j], diff_subln_g[j],
                                       diff_w_o[j], diff_lambda_init(i))
        x = x + rmsnorm(m, mix_post_g[i])
        h = rmsnorm(x, ffn_pre_g[i])
        f = conv_ffn(h, ffn_w_up[i], ffn_conv_w[i], ffn_conv_b[i], ffn_w_down[i])
        x = x + rmsnorm(f, ffn_post_g[i])
    return x
```

```python
import functools
import math

import jax
import jax.numpy as jnp
from jax import lax
from jax.experimental import pallas as pl
from jax.experimental.pallas import tpu as pltpu

D_MODEL = 1024
EPS = 1e-6
BLOCK = 128
SWA_HEADS = 16
SWA_KV_HEADS = 2
SWA_HEAD_DIM = 64
SWA_GROUP = SWA_HEADS // SWA_KV_HEADS
DIFF_HEADS = 8
DIFF_HEAD_DIM = 64
D_FF = 2816
CONV_WIDTH = 3
N_MIXERS = 2

VMEM_LIMIT_BYTES = 56 * 1024 * 1024
HALO = 16
NEG_BIG = -1e30

ROW_TILE = 512
FF_CHUNK = 256
DIFF_TQ = 512
DIFF_TK = 512


def _rms(x, g):
    return x * lax.rsqrt(jnp.mean(x * x, axis=-1, keepdims=True) + EPS) * g


def _params(sem):
    return pltpu.CompilerParams(dimension_semantics=sem,
                                vmem_limit_bytes=VMEM_LIMIT_BYTES)


def _const_spec(shape):
    nd = len(shape)
    return pl.BlockSpec(shape, lambda *_: (0,) * nd)


def _norm_matmul_kernel(x_ref, g_ref, w_ref, o_ref, *, n_chunk):
    h = _rms(x_ref[...], g_ref[...]).astype(jnp.bfloat16)
    n = o_ref.shape[1]
    for c in range(0, n, n_chunk):
        o_ref[:, c:c + n_chunk] = jnp.dot(
            h, w_ref[:, c:c + n_chunk],
            preferred_element_type=jnp.float32).astype(o_ref.dtype)


def norm_matmul(x2d, g, w_bf16, n_chunk=512):
    m, d = x2d.shape
    n = w_bf16.shape[1]
    return pl.pallas_call(
        functools.partial(_norm_matmul_kernel, n_chunk=n_chunk),
        grid=(m // ROW_TILE,),
        in_specs=[pl.BlockSpec((ROW_TILE, d), lambda i: (i, 0)),
                  _const_spec((1, d)),
                  _const_spec((d, n))],
        out_specs=pl.BlockSpec((ROW_TILE, n), lambda i: (i, 0)),
        out_shape=jax.ShapeDtypeStruct((m, n), jnp.bfloat16),
        compiler_params=_params(("parallel",)),
        name="norm_matmul",
    )(x2d, g.reshape(1, d), w_bf16)


def _matmul_norm_res_kernel(a_ref, w_ref, g_ref, x_ref, o_ref):
    y = jnp.dot(a_ref[...], w_ref[...], preferred_element_type=jnp.float32)
    o_ref[...] = x_ref[...] + _rms(y, g_ref[...])


def matmul_norm_residual(a_bf16, w_bf16, g, x2d):
    m, k = a_bf16.shape
    d = w_bf16.shape[1]
    return pl.pallas_call(
        _matmul_norm_res_kernel,
        grid=(m // ROW_TILE,),
        in_specs=[pl.BlockSpec((ROW_TILE, k), lambda i: (i, 0)),
                  _const_spec((k, d)),
                  _const_spec((1, d)),
                  pl.BlockSpec((ROW_TILE, d), lambda i: (i, 0))],
        out_specs=pl.BlockSpec((ROW_TILE, d), lambda i: (i, 0)),
        out_shape=jax.ShapeDtypeStruct((m, d), jnp.float32),
        compiler_params=_params(("parallel",)),
        name="matmul_norm_residual",
    )(a_bf16, w_bf16, g.reshape(1, d), x2d)


def _ffn_up_kernel(halo_ref, x_ref, g_ref, wup_ref, cw_ref, cb_ref, o_ref,
                   h_scr, ug_scr, uv_scr, *, seq_len):
    i = pl.program_id(0)
    tm = x_ref.shape[0]
    first = (i * tm) % seq_len == 0
    halo = jnp.where(first, 0.0, halo_ref[...])
    g = g_ref[...]
    h_scr[0:HALO, :] = _rms(halo, g).astype(jnp.bfloat16)
    h_scr[HALO:, :] = _rms(x_ref[...], g).astype(jnp.bfloat16)
    h = h_scr[...]

    def conv(u_scr, col):
        w = cw_ref[:, col:col + FF_CHUNK]
        c = cb_ref[:, col:col + FF_CHUNK]
        for tap in range(CONV_WIDTH):
            off = HALO - (CONV_WIDTH - 1) + tap
            c = c + w[tap:tap + 1, :] * u_scr[off:off + tm, :]
        return c

    for j in range(D_FF // FF_CHUNK):
        cg = j * FF_CHUNK
        cv = D_FF + cg
        ug_scr[...] = jnp.dot(h, wup_ref[:, cg:cg + FF_CHUNK],
                              preferred_element_type=jnp.float32)
        uv_scr[...] = jnp.dot(h, wup_ref[:, cv:cv + FF_CHUNK],
                              preferred_element_type=jnp.float32)
        gate = conv(ug_scr, cg)
        val = conv(uv_scr, cv)
        o_ref[:, cg:cg + FF_CHUNK] = (
            jax.nn.gelu(gate, approximate=True) * val).astype(o_ref.dtype)


def ffn_up(x2d, g, wup_bf16, conv_w, conv_b, seq_len):
    m, d = x2d.shape
    n2 = wup_bf16.shape[1]
    halo_blocks = ROW_TILE // HALO
    return pl.pallas_call(
        functools.partial(_ffn_up_kernel, seq_len=seq_len),
        grid=(m // ROW_TILE,),
        in_specs=[pl.BlockSpec((HALO, d),
                               lambda i: (jnp.maximum(i * halo_blocks - 1, 0), 0)),
                  pl.BlockSpec((ROW_TILE, d), lambda i: (i, 0)),
                  _const_spec((1, d)),
                  _const_spec((d, n2)),
                  _const_spec((CONV_WIDTH, n2)),
                  _const_spec((1, n2))],
        out_specs=pl.BlockSpec((ROW_TILE, D_FF), lambda i: (i, 0)),
        out_shape=jax.ShapeDtypeStruct((m, D_FF), jnp.bfloat16),
        scratch_shapes=[pltpu.VMEM((ROW_TILE + HALO, d), jnp.bfloat16),
                        pltpu.VMEM((ROW_TILE + HALO, FF_CHUNK), jnp.float32),
                        pltpu.VMEM((ROW_TILE + HALO, FF_CHUNK), jnp.float32)],
        compiler_params=_params(("parallel",)),
        name="ffn_up",
    )(x2d, x2d, g.reshape(1, d), wup_bf16, conv_w, conv_b.reshape(1, n2))


def _swa_kernel(slopes_ref, sinks_ref, q_ref, kc_ref, kp_ref, vc_ref, vp_ref,
                o_ref):
    n = pl.program_id(1)
    qi = lax.broadcasted_iota(jnp.int32, (BLOCK, 2 * BLOCK), 0)
    kj = lax.broadcasted_iota(jnp.int32, (BLOCK, 2 * BLOCK), 1)
    dist = qi + BLOCK - kj
    first_key = jnp.where(n > 0, 0, BLOCK)
    valid = (dist >= 0) & (dist < BLOCK) & (kj >= first_key)
    dist_f = dist.astype(jnp.float32)
    k_all = jnp.concatenate([kp_ref[...], kc_ref[...]], axis=0)
    v_all = jnp.concatenate([vp_ref[...], vc_ref[...]], axis=0)
    scale = SWA_HEAD_DIM ** -0.5
    for kv in range(SWA_KV_HEADS):
        lo = kv * SWA_HEAD_DIM
        k = k_all[:, lo:lo + SWA_HEAD_DIM]
        v = v_all[:, lo:lo + SWA_HEAD_DIM]
        for gidx in range(SWA_GROUP):
            hd = kv * SWA_GROUP + gidx
            q = q_ref[:, hd * SWA_HEAD_DIM:(hd + 1) * SWA_HEAD_DIM]
            s = lax.dot_general(q, k, (((1,), (1,)), ((), ())),
                                preferred_element_type=jnp.float32) * scale
            s = jnp.where(valid, s - slopes_ref[hd] * dist_f, NEG_BIG)
            sink = sinks_ref[hd]
            mx = jnp.maximum(jnp.max(s, axis=-1, keepdims=True), sink)
            p = jnp.exp(s - mx)
            denom = jnp.sum(p, axis=-1, keepdims=True) + jnp.exp(sink - mx)
            pv = jnp.dot((p / denom).astype(jnp.bfloat16), v,
                         preferred_element_type=jnp.float32)
            o_ref[:, hd * SWA_HEAD_DIM:(hd + 1) * SWA_HEAD_DIM] = (
                pv.astype(o_ref.dtype))


def swa_attention(qkv, slopes, sinks, batch, seq_len):
    m = qkv.shape[0]
    nb = seq_len // BLOCK
    hq = SWA_HEADS * SWA_HEAD_DIM
    kv_w = SWA_KV_HEADS * SWA_HEAD_DIM
    k_col = hq // kv_w
    v_col = k_col + 1

    def cur(col):
        return lambda b, n: (b * nb + n, col)

    def prev(col):
        return lambda b, n: (b * nb + jnp.maximum(n - 1, 0), col)

    smem = pl.BlockSpec(memory_space=pltpu.SMEM)
    return pl.pallas_call(
        _swa_kernel,
        grid=(batch, nb),
        in_specs=[smem, smem,
                  pl.BlockSpec((BLOCK, hq), cur(0)),
                  pl.BlockSpec((BLOCK, kv_w), cur(k_col)),
                  pl.BlockSpec((BLOCK, kv_w), prev(k_col)),
                  pl.BlockSpec((BLOCK, kv_w), cur(v_col)),
                  pl.BlockSpec((BLOCK, kv_w), prev(v_col))],
        out_specs=pl.BlockSpec((BLOCK, hq), cur(0)),
        out_shape=jax.ShapeDtypeStruct((m, hq), jnp.bfloat16),
        compiler_params=_params(("parallel", "parallel")),
        name="swa_attention",
    )(slopes, sinks, qkv, qkv, qkv, qkv, qkv)


def _diff_kernel(slopes_ref, lamv_ref, subg_ref, q_ref, k_ref, v_ref, o_ref,
                 m_scr, l_scr, acc_scr, *, lambda_init):
    hd = pl.program_id(1)
    qt = pl.program_id(2)
    tq, tk, d = DIFF_TQ, DIFF_TK, DIFF_HEAD_DIM
    slope = slopes_ref[hd]
    scale = d ** -0.5

    lane = lax.broadcasted_iota(jnp.int32, (tq, 2 * d), 1)
    q = q_ref[...]
    zero = jnp.zeros_like(q)
    q_stack = jnp.concatenate([jnp.where(lane < d, q, zero),
                               jnp.where(lane >= d, q, zero)], axis=0)

    m_scr[...] = jnp.full(m_scr.shape, NEG_BIG, jnp.float32)
    l_scr[...] = jnp.zeros(l_scr.shape, jnp.float32)
    acc_scr[...] = jnp.zeros(acc_scr.shape, jnp.float32)

    col = lax.broadcasted_iota(jnp.int32, (1, tk), 1)

    def step(j, masked):
        k0 = pl.multiple_of(j * tk, tk)
        k = k_ref[pl.ds(k0, tk), :]
        v = v_ref[pl.ds(k0, tk), :]
        s = lax.dot_general(q_stack, k, (((1,), (1,)), ((), ())),
                            preferred_element_type=jnp.float32) * scale
        s = s + slope * (col + (k0 - qt * tq)).astype(jnp.float32)
        if masked:
            ri = lax.broadcasted_iota(jnp.int32, (2 * tq, tk), 0)
            ri = jnp.where(ri >= tq, ri - tq, ri)
            ci = lax.broadcasted_iota(jnp.int32, (2 * tq, tk), 1)
            s = jnp.where(ri >= ci, s, NEG_BIG)
        m_old = m_scr[...]
        m_new = jnp.maximum(m_old, jnp.max(s, axis=-1, keepdims=True))
        alpha = jnp.exp(m_old - m_new)
        p = jnp.exp(s - m_new)
        l_scr[...] = alpha * l_scr[...] + jnp.sum(p, axis=-1, keepdims=True)
        acc_scr[...] = alpha * acc_scr[...] + jnp.dot(
            p.astype(jnp.bfloat16), v, preferred_element_type=jnp.float32)
        m_scr[...] = m_new

    def body(j, carry):
        step(j, masked=False)
        return carry

    lax.fori_loop(0, qt, body, 0)
    step(qt, masked=True)

    lam = (jnp.exp(jnp.sum(lamv_ref[0:1, :] * lamv_ref[1:2, :]))
           - jnp.exp(jnp.sum(lamv_ref[2:3, :] * lamv_ref[3:4, :]))
           + lambda_init)
    o_all = acc_scr[...] / l_scr[...]
    o = o_all[0:tq, :] - lam * o_all[tq:, :]
    o = _rms(o, subg_ref[...]) * (1.0 - lambda_init)
    o_ref[...] = o.astype(o_ref.dtype)


def diff_attention(qkv, slopes, lam_vecs, subln_g, lambda_init, batch, seq_len):
    m = qkv.shape[0]
    hw = 2 * DIFF_HEAD_DIM
    nq = seq_len // DIFF_TQ
    smem = pl.BlockSpec(memory_space=pltpu.SMEM)
    return pl.pallas_call(
        functools.partial(_diff_kernel, lambda_init=lambda_init),
        grid=(batch, DIFF_HEADS, nq),
        in_specs=[smem,
                  _const_spec(lam_vecs.shape),
                  _const_spec((1, hw)),
                  pl.BlockSpec((DIFF_TQ, hw), lambda b, h, i: (b * nq + i, h)),
                  pl.BlockSpec((seq_len, hw), lambda b, h, i: (b, DIFF_HEADS + h)),
                  pl.BlockSpec((seq_len, hw),
                               lambda b, h, i: (b, 2 * DIFF_HEADS + h))],
        out_specs=pl.BlockSpec((DIFF_TQ, hw), lambda b, h, i: (b * nq + i, h)),
        out_shape=jax.ShapeDtypeStruct((m, DIFF_HEADS * hw), jnp.bfloat16),
        scratch_shapes=[pltpu.VMEM((2 * DIFF_TQ, 1), jnp.float32),
                        pltpu.VMEM((2 * DIFF_TQ, 1), jnp.float32),
                        pltpu.VMEM((2 * DIFF_TQ, hw), jnp.float32)],
        compiler_params=_params(("parallel", "parallel", "parallel")),
        name="diff_attention",
    )(slopes, lam_vecs, subln_g.reshape(1, hw), qkv, qkv, qkv)


def _alibi_slopes(n_heads):
    return jnp.exp2(-8.0 * (jnp.arange(n_heads, dtype=jnp.float32) + 1.0) / n_heads)


def _diff_lambda_init(layer_idx):
    return 0.8 - 0.6 * math.exp(-0.3 * layer_idx)


def kernel(x, mix_pre_g, mix_post_g, ffn_pre_g, ffn_post_g, swa_w_qkv, swa_sinks, swa_w_o, diff_w_qkv, diff_lam_q1, diff_lam_k1, diff_lam_q2, diff_lam_k2, diff_subln_g, diff_w_o, ffn_w_up, ffn_conv_w, ffn_conv_b, ffn_w_down):
    batch, seq_len, d = x.shape
    depth = mix_pre_g.shape[0]
    bf = jnp.bfloat16
    x2d = x.reshape(batch * seq_len, d)
    for i in range(depth):
        j = i // N_MIXERS
        if i % N_MIXERS == 0:
            qkv = norm_matmul(x2d, mix_pre_g[i], swa_w_qkv[j].astype(bf),
                              n_chunk=640)
            a = swa_attention(qkv, _alibi_slopes(SWA_HEADS),
                              swa_sinks[j].astype(jnp.float32), batch, seq_len)
            w_o = swa_w_o[j]
        else:
            qkv = norm_matmul(x2d, mix_pre_g[i], diff_w_qkv[j].astype(bf))
            lam_vecs = jnp.stack([diff_lam_q1[j], diff_lam_k1[j],
                                  diff_lam_q2[j], diff_lam_k2[j]]).astype(jnp.float32)
            a = diff_attention(qkv, _alibi_slopes(DIFF_HEADS), lam_vecs,
                               diff_subln_g[j], _diff_lambda_init(i),
                               batch, seq_len)
            w_o = diff_w_o[j]
        x2d = matmul_norm_residual(a, w_o.astype(bf), mix_post_g[i], x2d)
        act = ffn_up(x2d, ffn_pre_g[i], ffn_w_up[i].astype(bf), ffn_conv_w[i],
                     ffn_conv_b[i], seq_len)
        x2d = matmul_norm_residual(act, ffn_w_down[i].astype(bf), ffn_post_g[i], x2d)
    return x2d.reshape(batch, seq_len, d)
```

```python
import functools
import math

import jax
import jax.numpy as jnp
from jax import lax
from jax.experimental import pallas as pl
from jax.experimental.pallas import tpu as pltpu

D_MODEL = 1024
EPS = 1e-6
BLOCK = 128
SWA_HEADS = 16
SWA_KV_HEADS = 2
SWA_HEAD_DIM = 64
SWA_GROUP = SWA_HEADS // SWA_KV_HEADS
DIFF_HEADS = 8
DIFF_HEAD_DIM = 64
D_FF = 2816
CONV_WIDTH = 3
N_MIXERS = 2

VMEM_LIMIT_BYTES = 56 * 1024 * 1024
HALO = 16
NEG_BIG = -1e30

ROW_TILE = 512
FF_CHUNK = 256
DIFF_TQ = 512
DIFF_TK = 512
DIFF_VROWS = 144
LOG2E = 1.4426950408889634


def _rms(x, g):
    return x * lax.rsqrt(jnp.mean(x * x, axis=-1, keepdims=True) + EPS) * g


def _params(sem):
    return pltpu.CompilerParams(dimension_semantics=sem,
                                vmem_limit_bytes=VMEM_LIMIT_BYTES)


def _const_spec(shape):
    nd = len(shape)
    return pl.BlockSpec(shape, lambda *_: (0,) * nd)


def _norm_matmul_kernel(x_ref, g_ref, w_ref, o_ref, *, n_chunk):
    h = _rms(x_ref[...], g_ref[...]).astype(jnp.bfloat16)
    n = o_ref.shape[1]
    for c in range(0, n, n_chunk):
        o_ref[:, c:c + n_chunk] = jnp.dot(
            h, w_ref[:, c:c + n_chunk],
            preferred_element_type=jnp.float32).astype(o_ref.dtype)


def norm_matmul(x2d, g, w_bf16, n_chunk=512):
    m, d = x2d.shape
    n = w_bf16.shape[1]
    return pl.pallas_call(
        functools.partial(_norm_matmul_kernel, n_chunk=n_chunk),
        grid=(m // ROW_TILE,),
        in_specs=[pl.BlockSpec((ROW_TILE, d), lambda i: (i, 0)),
                  _const_spec((1, d)),
                  _const_spec((d, n))],
        out_specs=pl.BlockSpec((ROW_TILE, n), lambda i: (i, 0)),
        out_shape=jax.ShapeDtypeStruct((m, n), jnp.bfloat16),
        compiler_params=_params(("parallel",)),
        name="norm_matmul",
    )(x2d, g.reshape(1, d), w_bf16)


def _matmul_norm_res_kernel(a_ref, w_ref, g_ref, x_ref, o_ref):
    y = jnp.dot(a_ref[...], w_ref[...], preferred_element_type=jnp.float32)
    o_ref[...] = x_ref[...] + _rms(y, g_ref[...])


def matmul_norm_residual(a_bf16, w_bf16, g, x2d):
    m, k = a_bf16.shape
    d = w_bf16.shape[1]
    return pl.pallas_call(
        _matmul_norm_res_kernel,
        grid=(m // ROW_TILE,),
        in_specs=[pl.BlockSpec((ROW_TILE, k), lambda i: (i, 0)),
                  _const_spec((k, d)),
                  _const_spec((1, d)),
                  pl.BlockSpec((ROW_TILE, d), lambda i: (i, 0))],
        out_specs=pl.BlockSpec((ROW_TILE, d), lambda i: (i, 0)),
        out_shape=jax.ShapeDtypeStruct((m, d), jnp.float32),
        compiler_params=_params(("parallel",)),
        name="matmul_norm_residual",
    )(a_bf16, w_bf16, g.reshape(1, d), x2d)


def _ffn_up_kernel(halo_ref, x_ref, g_ref, wup_ref, cw_ref, cb_ref, o_ref,
                   h_scr, ug_scr, uv_scr, *, seq_len):
    i = pl.program_id(0)
    tm = x_ref.shape[0]
    first = (i * tm) % seq_len == 0
    halo = jnp.where(first, 0.0, halo_ref[...])
    g = g_ref[...]
    h_scr[0:HALO, :] = _rms(halo, g).astype(jnp.bfloat16)
    h_scr[HALO:, :] = _rms(x_ref[...], g).astype(jnp.bfloat16)
    h = h_scr[...]

    def conv(u_scr, col):
        w = cw_ref[:, col:col + FF_CHUNK]
        c = cb_ref[:, col:col + FF_CHUNK]
        for tap in range(CONV_WIDTH):
            off = HALO - (CONV_WIDTH - 1) + tap
            c = c + w[tap:tap + 1, :] * u_scr[off:off + tm, :]
        return c

    for j in range(D_FF // FF_CHUNK):
        cg = j * FF_CHUNK
        cv = D_FF + cg
        ug_scr[...] = jnp.dot(h, wup_ref[:, cg:cg + FF_CHUNK],
                              preferred_element_type=jnp.float32)
        uv_scr[...] = jnp.dot(h, wup_ref[:, cv:cv + FF_CHUNK],
                              preferred_element_type=jnp.float32)
        gate = conv(ug_scr, cg)
        val = conv(uv_scr, cv)
        o_ref[:, cg:cg + FF_CHUNK] = (
            jax.nn.gelu(gate, approximate=True) * val).astype(o_ref.dtype)


def ffn_up(x2d, g, wup_bf16, conv_w, conv_b, seq_len):
    m, d = x2d.shape
    n2 = wup_bf16.shape[1]
    halo_blocks = ROW_TILE // HALO
    return pl.pallas_call(
        functools.partial(_ffn_up_kernel, seq_len=seq_len),
        grid=(m // ROW_TILE,),
        in_specs=[pl.BlockSpec((HALO, d),
                               lambda i: (jnp.maximum(i * halo_blocks - 1, 0), 0)),
                  pl.BlockSpec((ROW_TILE, d), lambda i: (i, 0)),
                  _const_spec((1, d)),
                  _const_spec((d, n2)),
                  _const_spec((CONV_WIDTH, n2)),
                  _const_spec((1, n2))],
        out_specs=pl.BlockSpec((ROW_TILE, D_FF), lambda i: (i, 0)),
        out_shape=jax.ShapeDtypeStruct((m, D_FF), jnp.bfloat16),
        scratch_shapes=[pltpu.VMEM((ROW_TILE + HALO, d), jnp.bfloat16),
                        pltpu.VMEM((ROW_TILE + HALO, FF_CHUNK), jnp.float32),
                        pltpu.VMEM((ROW_TILE + HALO, FF_CHUNK), jnp.float32)],
        compiler_params=_params(("parallel",)),
        name="ffn_up",
    )(x2d, x2d, g.reshape(1, d), wup_bf16, conv_w, conv_b.reshape(1, n2))


def _swa_kernel(slopes_ref, sinks_ref, q_ref, kc_ref, kp_ref, vc_ref, vp_ref,
                o_ref):
    n = pl.program_id(1)
    qi = lax.broadcasted_iota(jnp.int32, (BLOCK, 2 * BLOCK), 0)
    kj = lax.broadcasted_iota(jnp.int32, (BLOCK, 2 * BLOCK), 1)
    dist = qi + BLOCK - kj
    first_key = jnp.where(n > 0, 0, BLOCK)
    valid = (dist >= 0) & (dist < BLOCK) & (kj >= first_key)
    dist_f = dist.astype(jnp.float32)
    k_all = jnp.concatenate([kp_ref[...], kc_ref[...]], axis=0)
    v_all = jnp.concatenate([vp_ref[...], vc_ref[...]], axis=0)
    scale = SWA_HEAD_DIM ** -0.5
    for kv in range(SWA_KV_HEADS):
        lo = kv * SWA_HEAD_DIM
        k = k_all[:, lo:lo + SWA_HEAD_DIM]
        v = v_all[:, lo:lo + SWA_HEAD_DIM]
        for gidx in range(SWA_GROUP):
            hd = kv * SWA_GROUP + gidx
            q = q_ref[:, hd * SWA_HEAD_DIM:(hd + 1) * SWA_HEAD_DIM]
            s = lax.dot_general(q, k, (((1,), (1,)), ((), ())),
                                preferred_element_type=jnp.float32) * scale
            s = jnp.where(valid, s - slopes_ref[hd] * dist_f, NEG_BIG)
            sink = sinks_ref[hd]
            mx = jnp.maximum(jnp.max(s, axis=-1, keepdims=True), sink)
            p = jnp.exp(s - mx)
            denom = jnp.sum(p, axis=-1, keepdims=True) + jnp.exp(sink - mx)
            pv = jnp.dot((p / denom).astype(jnp.bfloat16), v,
                         preferred_element_type=jnp.float32)
            o_ref[:, hd * SWA_HEAD_DIM:(hd + 1) * SWA_HEAD_DIM] = (
                pv.astype(o_ref.dtype))


def swa_attention(qkv, slopes, sinks, batch, seq_len):
    m = qkv.shape[0]
    nb = seq_len // BLOCK
    hq = SWA_HEADS * SWA_HEAD_DIM
    kv_w = SWA_KV_HEADS * SWA_HEAD_DIM
    k_col = hq // kv_w
    v_col = k_col + 1

    def cur(col):
        return lambda b, n: (b * nb + n, col)

    def prev(col):
        return lambda b, n: (b * nb + jnp.maximum(n - 1, 0), col)

    smem = pl.BlockSpec(memory_space=pltpu.SMEM)
    return pl.pallas_call(
        _swa_kernel,
        grid=(batch, nb),
        in_specs=[smem, smem,
                  pl.BlockSpec((BLOCK, hq), cur(0)),
                  pl.BlockSpec((BLOCK, kv_w), cur(k_col)),
                  pl.BlockSpec((BLOCK, kv_w), prev(k_col)),
                  pl.BlockSpec((BLOCK, kv_w), cur(v_col)),
                  pl.BlockSpec((BLOCK, kv_w), prev(v_col))],
        out_specs=pl.BlockSpec((BLOCK, hq), cur(0)),
        out_shape=jax.ShapeDtypeStruct((m, hq), jnp.bfloat16),
        compiler_params=_params(("parallel", "parallel")),
        name="swa_attention",
    )(slopes, sinks, qkv, qkv, qkv, qkv, qkv)


def _diff_kernel(slopes_ref, lamv_ref, subg_ref, q_ref, k_ref, v_ref, o_ref,
                 kaug, vt, qaug_t, acc_scr, m_scr, s_a, s_b, *, lambda_init,
                 seq_len):
    hd = pl.program_id(1)
    qi = pl.program_id(2)
    tq, tk, d = DIFF_TQ, DIFF_TK, DIFF_HEAD_DIM
    hw = 2 * d
    f32, bf16 = jnp.float32, jnp.bfloat16

    @pl.when(qi == 0)
    def _stage_keys_values():
        lane = lax.broadcasted_iota(jnp.int32, (tk, hw), 1)
        row = lax.broadcasted_iota(jnp.int32, (tk, hw), 0)
        for c in range(seq_len // tk):
            r0 = c * tk
            pos = row + r0
            hi = (pos >> 6).astype(f32)
            lo = (pos & 63).astype(f32)
            aux = jnp.where(lane < 3, hi,
                            jnp.where(lane < 6, lo,
                                      jnp.where(lane < 9, 1.0, 0.0)))
            kaug[c, :, 0:hw] = k_ref[r0:r0 + tk, :]
            kaug[c, :, hw:2 * hw] = aux.astype(bf16)
            vt[c, 0:hw, :] = v_ref[r0:r0 + tk, :].astype(f32).T.astype(bf16)
            vt[c, hw:DIFF_VROWS, :] = jnp.ones((DIFF_VROWS - hw, tk), bf16)

    q_t = (q_ref[...].astype(f32) * (LOG2E * d ** -0.5)).T
    qrow = lax.broadcasted_iota(jnp.int32, (hw, tq), 0)
    qaug_t[0:hw, 0:tq] = jnp.where(qrow < d, q_t, 0.0).astype(bf16)
    qaug_t[0:hw, tq:2 * tq] = jnp.where(qrow >= d, q_t, 0.0).astype(bf16)
    a_coef = slopes_ref[hd] * LOG2E
    c_coef = -a_coef * (qi * tq).astype(f32)
    r = lax.broadcasted_iota(jnp.int32, (16, 2 * tq), 0)
    t = jnp.where(r < 3, a_coef * 64.0,
                  jnp.where(r < 6, a_coef, jnp.where(r < 9, c_coef, 0.0)))
    t0 = t.astype(bf16).astype(f32)
    t1 = (t - t0).astype(bf16).astype(f32)
    t2 = t - t0 - t1
    first = (r == 0) | (r == 3) | (r == 6)
    second = (r == 1) | (r == 4) | (r == 7)
    qaug_t[hw:hw + 16, :] = jnp.where(first, t0,
                                      jnp.where(second, t1, t2)).astype(bf16)
    qaug_t[hw + 16:2 * hw, :] = jnp.zeros((hw - 16, 2 * tq), bf16)

    acc_scr[...] = jnp.zeros(acc_scr.shape, f32)

    m_scr[...] = jnp.full(m_scr.shape, NEG_BIG, f32)

    def scores(j, dst):
        dst[...] = jnp.dot(kaug[j], qaug_t[...], preferred_element_type=f32)

    def softmax_pv(j, src, masked):
        s = src[...]
        if masked:
            key = lax.broadcasted_iota(jnp.int32, (tk, 2 * tq), 0)
            qry = lax.broadcasted_iota(jnp.int32, (tk, 2 * tq), 1)
            qry = jnp.where(qry >= tq, qry - tq, qry)
            s = jnp.where(key <= qry, s, NEG_BIG)
        m_old = m_scr[...]
        m_new = jnp.maximum(m_old, jnp.max(s, axis=0, keepdims=True))
        alpha = jnp.exp2(m_old - m_new)
        p = jnp.exp2(s - m_new).astype(bf16)
        acc_scr[...] = acc_scr[...] * alpha + jnp.dot(
            vt[j], p, preferred_element_type=f32)
        m_scr[...] = m_new

    def stage(j, src, dst):
        scores(j + 1, dst)
        softmax_pv(j, src, masked=False)

    scores(0, s_a)

    def pair(i, carry):
        stage(2 * i, s_a, s_b)
        stage(2 * i + 1, s_b, s_a)
        return carry

    lax.fori_loop(0, qi // 2, pair, 0)

    @pl.when(qi % 2 == 1)
    def _odd_tail():
        stage(qi - 1, s_a, s_b)
        softmax_pv(qi, s_b, masked=True)

    @pl.when(qi % 2 == 0)
    def _even_tail():
        softmax_pv(qi, s_a, masked=True)

    lam = (jnp.exp(jnp.sum(lamv_ref[0:1, :] * lamv_ref[1:2, :]))
           - jnp.exp(jnp.sum(lamv_ref[2:3, :] * lamv_ref[3:4, :]))
           + lambda_init)
    acc = acc_scr[...]
    o_all = acc[0:hw, :] / acc[hw:hw + 1, :]
    o_t = o_all[:, 0:tq] - lam * o_all[:, tq:2 * tq]
    o = _rms(o_t.T, subg_ref[...]) * (1.0 - lambda_init)
    o_ref[...] = o.astype(o_ref.dtype)


def diff_attention(qkv, slopes, lam_vecs, subln_g, lambda_init, batch, seq_len):
    m = qkv.shape[0]
    hw = 2 * DIFF_HEAD_DIM
    nq = seq_len // DIFF_TQ
    smem = pl.BlockSpec(memory_space=pltpu.SMEM)
    assert DIFF_TQ == DIFF_TK and seq_len % DIFF_TK == 0
    nk = seq_len // DIFF_TK
    return pl.pallas_call(
        functools.partial(_diff_kernel, lambda_init=lambda_init,
                          seq_len=seq_len),
        grid=(batch, DIFF_HEADS, nq),
        in_specs=[smem,
                  _const_spec(lam_vecs.shape),
                  _const_spec((1, hw)),
                  pl.BlockSpec((DIFF_TQ, hw), lambda b, h, i: (b * nq + i, h)),
                  pl.BlockSpec((seq_len, hw), lambda b, h, i: (b, DIFF_HEADS + h)),
                  pl.BlockSpec((seq_len, hw),
                               lambda b, h, i: (b, 2 * DIFF_HEADS + h))],
        out_specs=pl.BlockSpec((DIFF_TQ, hw), lambda b, h, i: (b * nq + i, h)),
        out_shape=jax.ShapeDtypeStruct((m, DIFF_HEADS * hw), jnp.bfloat16),
        scratch_shapes=[pltpu.VMEM((nk, DIFF_TK, 2 * hw), jnp.bfloat16),
                        pltpu.VMEM((nk, DIFF_VROWS, DIFF_TK), jnp.bfloat16),
                        pltpu.VMEM((2 * hw, 2 * DIFF_TQ), jnp.bfloat16),
                        pltpu.VMEM((DIFF_VROWS, 2 * DIFF_TQ), jnp.float32),
                        pltpu.VMEM((1, 2 * DIFF_TQ), jnp.float32),
                        pltpu.VMEM((DIFF_TK, 2 * DIFF_TQ), jnp.float32),
                        pltpu.VMEM((DIFF_TK, 2 * DIFF_TQ), jnp.float32)],
        compiler_params=_params(("parallel", "parallel", "arbitrary")),
        name="diff_attention",
    )(slopes, lam_vecs, subln_g.reshape(1, hw), qkv, qkv, qkv)


def _alibi_slopes(n_heads):
    return jnp.exp2(-8.0 * (jnp.arange(n_heads, dtype=jnp.float32) + 1.0) / n_heads)


def _diff_lambda_init(layer_idx):
    return 0.8 - 0.6 * math.exp(-0.3 * layer_idx)


def kernel(x, mix_pre_g, mix_post_g, ffn_pre_g, ffn_post_g, swa_w_qkv, swa_sinks, swa_w_o, diff_w_qkv, diff_lam_q1, diff_lam_k1, diff_lam_q2, diff_lam_k2, diff_subln_g, diff_w_o, ffn_w_up, ffn_conv_w, ffn_conv_b, ffn_w_down):
    batch, seq_len, d = x.shape
    depth = mix_pre_g.shape[0]
    bf = jnp.bfloat16
    x2d = x.reshape(batch * seq_len, d)
    for i in range(depth):
        j = i // N_MIXERS
        if i % N_MIXERS == 0:
            qkv = norm_matmul(x2d, mix_pre_g[i], swa_w_qkv[j].astype(bf),
                              n_chunk=640)
            a = swa_attention(qkv, _alibi_slopes(SWA_HEADS),
                              swa_sinks[j].astype(jnp.float32), batch, seq_len)
            w_o = swa_w_o[j]
        else:
            qkv = norm_matmul(x2d, mix_pre_g[i], diff_w_qkv[j].astype(bf))
            lam_vecs = jnp.stack([diff_lam_q1[j], diff_lam_k1[j],
                                  diff_lam_q2[j], diff_lam_k2[j]]).astype(jnp.float32)
            a = diff_attention(qkv, _alibi_slopes(DIFF_HEADS), lam_vecs,
                               diff_subln_g[j], _diff_lambda_init(i),
                               batch, seq_len)
            w_o = diff_w_o[j]
        x2d = matmul_norm_residual(a, w_o.astype(bf), mix_post_g[i], x2d)
        act = ffn_up(x2d, ffn_pre_g[i], ffn_w_up[i].astype(bf), ffn_conv_w[i],
                     ffn_conv_b[i], seq_len)
        x2d = matmul_norm_residual(act, ffn_w_down[i].astype(bf), ffn_post_g[i], x2d)
    return x2d.reshape(batch, seq_len, d)
```

```python
import functools
import math

import jax
import jax.numpy as jnp
from jax import lax
from jax.experimental import pallas as pl
from jax.experimental.pallas import tpu as pltpu

D_MODEL = 1024
EPS = 1e-6
BLOCK = 128
SWA_HEADS = 16
SWA_KV_HEADS = 2
SWA_HEAD_DIM = 64
SWA_GROUP = SWA_HEADS // SWA_KV_HEADS
DIFF_HEADS = 8
DIFF_HEAD_DIM = 64
D_FF = 2816
CONV_WIDTH = 3
N_MIXERS = 2

VMEM_LIMIT_BYTES = 56 * 1024 * 1024
LANES = 128
HALO = 16
GELU_C0 = math.sqrt(2.0 / math.pi)
GELU_C1 = 0.044715 * GELU_C0
NEG_BIG = -1e30

ROW_TILE = 512
FFN_ROW_TILE = 1024
FF_CHUNK = 256
SWA_TILE = 512
SWA_VROWS = 80
DIFF_TQ = 512
DIFF_TK = 512
DIFF_VROWS = 144
LOG2E = 1.4426950408889634


def _rms(x, g):
    return x * lax.rsqrt(jnp.mean(x * x, axis=-1, keepdims=True) + EPS) * g


def _params(sem):
    return pltpu.CompilerParams(dimension_semantics=sem,
                                vmem_limit_bytes=VMEM_LIMIT_BYTES)


def _const_spec(shape):
    nd = len(shape)
    return pl.BlockSpec(shape, lambda *_: (0,) * nd,
                        pipeline_mode=pl.Buffered(1))


def _norm_matmul_kernel(x_ref, g_ref, w_ref, o_ref, *, n_chunk):
    h = _rms(x_ref[...], g_ref[...]).astype(jnp.bfloat16)
    n = o_ref.shape[1]
    for c in range(0, n, n_chunk):
        o_ref[:, c:c + n_chunk] = jnp.dot(
            h, w_ref[:, c:c + n_chunk],
            preferred_element_type=jnp.float32).astype(o_ref.dtype)


def norm_matmul(x2d, g, w_bf16, n_chunk=512):
    m, d = x2d.shape
    n = w_bf16.shape[1]
    return pl.pallas_call(
        functools.partial(_norm_matmul_kernel, n_chunk=n_chunk),
        grid=(m // ROW_TILE,),
        in_specs=[pl.BlockSpec((ROW_TILE, d), lambda i: (i, 0)),
                  _const_spec((1, d)),
                  _const_spec((d, n))],
        out_specs=pl.BlockSpec((ROW_TILE, n), lambda i: (i, 0)),
        out_shape=jax.ShapeDtypeStruct((m, n), jnp.bfloat16),
        compiler_params=_params(("parallel",)),
        name="norm_matmul",
    )(x2d, g.reshape(1, d), w_bf16)


def _matmul_norm_res_kernel(a_ref, w_ref, g_ref, x_ref, o_ref):
    y = jnp.dot(a_ref[...], w_ref[...], preferred_element_type=jnp.float32)
    o_ref[...] = x_ref[...] + _rms(y, g_ref[...])


def matmul_norm_residual(a_bf16, w_bf16, g, x2d):
    m, k = a_bf16.shape
    d = w_bf16.shape[1]
    return pl.pallas_call(
        _matmul_norm_res_kernel,
        grid=(m // ROW_TILE,),
        in_specs=[pl.BlockSpec((ROW_TILE, k), lambda i: (i, 0)),
                  _const_spec((k, d)),
                  _const_spec((1, d)),
                  pl.BlockSpec((ROW_TILE, d), lambda i: (i, 0))],
        out_specs=pl.BlockSpec((ROW_TILE, d), lambda i: (i, 0)),
        out_shape=jax.ShapeDtypeStruct((m, d), jnp.float32),
        compiler_params=_params(("parallel",)),
        name="matmul_norm_residual",
    )(a_bf16, w_bf16, g.reshape(1, d), x2d)


def _ffn_up_kernel(halo_ref, x_ref, g_ref, wup_ref, cw_ref, cb_ref, o_ref,
                   h_scr, ug_scr, uv_scr, *, seq_len):
    i = pl.program_id(0)
    tm = x_ref.shape[0]
    first = (i * tm) % seq_len == 0
    halo = jnp.where(first, 0.0, halo_ref[...])
    g = g_ref[...]
    h_scr[0:HALO, :] = _rms(halo, g).astype(jnp.bfloat16)
    h_scr[HALO:, :] = _rms(x_ref[...], g).astype(jnp.bfloat16)

    def conv(u_scr, slab, col):
        c = cb_ref[:, col:col + LANES]
        for tap in range(CONV_WIDTH):
            off = HALO - (CONV_WIDTH - 1) + tap
            rows = pl.ds(off, tm) if off % 8 == 0 else pl.ds(off, tm, stride=1)
            c = c + cw_ref[tap:tap + 1, col:col + LANES] * u_scr[slab, rows, :]
        return c

    slabs = FF_CHUNK // LANES
    for j in range(D_FF // FF_CHUNK):
        cg = j * FF_CHUNK
        cv = D_FF + cg
        ug = jnp.dot(h_scr[...], wup_ref[:, cg:cg + FF_CHUNK],
                     preferred_element_type=jnp.float32)
        uv = jnp.dot(h_scr[...], wup_ref[:, cv:cv + FF_CHUNK],
                     preferred_element_type=jnp.float32)
        base = (j % 2) * slabs
        for s in range(slabs):
            ug_scr[base + s] = ug[:, s * LANES:(s + 1) * LANES]
            uv_scr[base + s] = uv[:, s * LANES:(s + 1) * LANES]
        for s in range(slabs):
            gate = conv(ug_scr, base + s, cg + s * LANES)
            val = conv(uv_scr, base + s, cv + s * LANES)
            t = jnp.tanh(gate * (GELU_C0 + GELU_C1 * (gate * gate)))
            o_ref[:, cg + s * LANES:cg + (s + 1) * LANES] = (
                (gate * val) * (0.5 * t + 0.5)).astype(o_ref.dtype)


def ffn_up(x2d, g, wup_bf16, conv_w, conv_b, seq_len):
    m, d = x2d.shape
    n2 = wup_bf16.shape[1]
    tm = FFN_ROW_TILE
    halo_blocks = tm // HALO
    u_scratch = pltpu.VMEM((2 * FF_CHUNK // LANES, tm + HALO, LANES), jnp.float32)
    return pl.pallas_call(
        functools.partial(_ffn_up_kernel, seq_len=seq_len),
        grid=(m // tm,),
        in_specs=[pl.BlockSpec((HALO, d),
                               lambda i: (jnp.maximum(i * halo_blocks - 1, 0), 0)),
                  pl.BlockSpec((tm, d), lambda i: (i, 0)),
                  _const_spec((1, d)),
                  _const_spec((d, n2)),
                  _const_spec((CONV_WIDTH, n2)),
                  _const_spec((1, n2))],
        out_specs=pl.BlockSpec((tm, D_FF), lambda i: (i, 0)),
        out_shape=jax.ShapeDtypeStruct((m, D_FF), jnp.bfloat16),
        scratch_shapes=[pltpu.VMEM((tm + HALO, d), jnp.bfloat16),
                        u_scratch, u_scratch],
        compiler_params=_params(("parallel",)),
        name="ffn_up",
    )(x2d, x2d, g.reshape(1, d), wup_bf16, conv_w, conv_b.reshape(1, n2))


def _swa_kernel(slopes_ref, sinks_ref, q_ref, k_ref, kp_ref, v_ref, vp_ref,
                o_ref, qt_all, k_aug, vt_aug, p_scr):
    t = pl.program_id(1)
    f32, bf16 = jnp.float32, jnp.bfloat16
    d, grp = SWA_HEAD_DIM, SWA_GROUP
    ncol = SWA_HEADS * BLOCK
    kvw = SWA_KV_HEADS * d

    @pl.when((pl.program_id(0) == 0) & (t == 0))
    def _init_constants():
        colh = lax.broadcasted_iota(jnp.int32, (16, ncol), 1) >> 7
        r = lax.broadcasted_iota(jnp.int32, (16, ncol), 0)
        a = jnp.zeros((16, ncol), f32)
        for h in range(SWA_HEADS):
            a = jnp.where(colh == h, slopes_ref[h] * LOG2E, a)
        a0 = a.astype(bf16).astype(f32)
        a1 = (a - a0).astype(bf16).astype(f32)
        a2 = a - a0 - a1
        coef = jnp.where(r == 0, a0, jnp.where(r == 1, a1,
                                               jnp.where(r == 2, a2, 0.0)))
        kj = lax.broadcasted_iota(jnp.int32, (2 * BLOCK, kvw), 0).astype(f32)
        lane = lax.broadcasted_iota(jnp.int32, (2 * BLOCK, kvw), 1)
        for par in range(SWA_TILE // BLOCK):
            qt_all[par] = jnp.zeros((2 * kvw, ncol), bf16)
            qt_all[par, kvw:kvw + 16, :] = coef.astype(bf16)
            k_aug[par, :, kvw:2 * kvw] = jnp.where(lane < 3, kj, 0.0).astype(bf16)
            for kv in range(SWA_KV_HEADS):
                vt_aug[par, kv, d:SWA_VROWS, :] = jnp.ones(
                    (SWA_VROWS - d, 2 * BLOCK), bf16)

    key = lax.broadcasted_iota(jnp.int32, (2 * BLOCK, BLOCK), 0)
    qry = lax.broadcasted_iota(jnp.int32, (2 * BLOCK, BLOCK), 1)
    dist = qry + BLOCK - key
    band = (dist >= 0) & (dist < BLOCK)
    band_first = band & (key >= jnp.where(t > 0, 0, BLOCK))
    qpos = (lax.broadcasted_iota(jnp.int32, (1, BLOCK), 1) + BLOCK).astype(f32)

    for blk in range(SWA_TILE // BLOCK):
        par = blk
        r0 = blk * BLOCK
        valid = band_first if blk == 0 else band
        k_prev = kp_ref[...] if blk == 0 else k_ref[r0 - BLOCK:r0, :]
        v_prev = vp_ref[...] if blk == 0 else v_ref[r0 - BLOCK:r0, :]
        k_aug[par, 0:BLOCK, 0:kvw] = k_prev
        k_aug[par, BLOCK:2 * BLOCK, 0:kvw] = k_ref[r0:r0 + BLOCK, :]
        v_t = jnp.concatenate([v_prev, v_ref[r0:r0 + BLOCK, :]],
                              axis=0).astype(f32).T
        for kv in range(SWA_KV_HEADS):
            vt_aug[par, kv, 0:d, :] = v_t[kv * d:(kv + 1) * d, :].astype(bf16)
        q_t = (q_ref[r0:r0 + BLOCK, :].astype(f32) * (LOG2E * d ** -0.5)).T
        for h in range(SWA_HEADS):
            kv = h // grp
            qt_all[par, kv * d:(kv + 1) * d, h * BLOCK:(h + 1) * BLOCK] = (
                q_t[h * d:(h + 1) * d, :].astype(bf16))

        s = jnp.dot(k_aug[par], qt_all[par], preferred_element_type=f32)
        m_list, sink_list = [], []
        for h in range(SWA_HEADS):
            s_h = jnp.where(valid, s[:, h * BLOCK:(h + 1) * BLOCK], NEG_BIG)
            sink_h = (sinks_ref[h] * LOG2E
                      + (slopes_ref[h] * LOG2E) * qpos)
            m_h = jnp.maximum(jnp.max(s_h, axis=0, keepdims=True), sink_h)
            p_scr[par, :, h * BLOCK:(h + 1) * BLOCK] = (
                jnp.exp2(s_h - m_h).astype(bf16))
            m_list.append(m_h)
            sink_list.append(sink_h)

        for kv in range(SWA_KV_HEADS):
            pv = jnp.dot(vt_aug[par, kv],
                         p_scr[par, :, kv * grp * BLOCK:(kv + 1) * grp * BLOCK],
                         preferred_element_type=f32)
            for pair in range(grp // 2):
                halves = []
                for g in (2 * pair, 2 * pair + 1):
                    h = kv * grp + g
                    cols = slice(g * BLOCK, (g + 1) * BLOCK)
                    denom = pv[d:d + 1, cols] + jnp.exp2(
                        sink_list[h] - m_list[h])
                    halves.append(pv[0:d, cols] / denom)
                o_pair = jnp.concatenate(halves, axis=0).T
                c0 = (kv * grp + 2 * pair) * d
                o_ref[r0:r0 + BLOCK, c0:c0 + 2 * d] = o_pair.astype(o_ref.dtype)


def swa_attention(qkv, slopes, sinks, batch, seq_len):
    m = qkv.shape[0]
    nt = seq_len // SWA_TILE
    per_tile = SWA_TILE // BLOCK
    hq = SWA_HEADS * SWA_HEAD_DIM
    kv_w = SWA_KV_HEADS * SWA_HEAD_DIM
    k_col = hq // kv_w
    v_col = k_col + 1
    ncol = SWA_HEADS * BLOCK

    def cur(col):
        return lambda b, t: (b * nt + t, col)

    def prev(col):
        return lambda b, t: (jnp.maximum((b * nt + t) * per_tile - 1, 0), col)

    smem = pl.BlockSpec(memory_space=pltpu.SMEM)
    return pl.pallas_call(
        _swa_kernel,
        grid=(batch, nt),
        in_specs=[smem, smem,
                  pl.BlockSpec((SWA_TILE, hq), cur(0)),
                  pl.BlockSpec((SWA_TILE, kv_w), cur(k_col)),
                  pl.BlockSpec((BLOCK, kv_w), prev(k_col)),
                  pl.BlockSpec((SWA_TILE, kv_w), cur(v_col)),
                  pl.BlockSpec((BLOCK, kv_w), prev(v_col))],
        out_specs=pl.BlockSpec((SWA_TILE, hq), cur(0)),
        out_shape=jax.ShapeDtypeStruct((m, hq), jnp.bfloat16),
        scratch_shapes=[pltpu.VMEM((per_tile, 2 * kv_w, ncol), jnp.bfloat16),
                        pltpu.VMEM((per_tile, 2 * BLOCK, 2 * kv_w), jnp.bfloat16),
                        pltpu.VMEM((per_tile, SWA_KV_HEADS, SWA_VROWS, 2 * BLOCK),
                                   jnp.bfloat16),
                        pltpu.VMEM((per_tile, 2 * BLOCK, ncol), jnp.bfloat16)],
        compiler_params=_params(("arbitrary", "arbitrary")),
        name="swa_attention",
    )(slopes, sinks, qkv, qkv, qkv, qkv, qkv)


def _diff_kernel(slopes_ref, lamv_ref, subg_ref, q_ref, k_ref, v_ref, o_ref,
                 kaug, vt, qaug_t, acc_scr, m_scr, s_a, s_b, *, lambda_init,
                 seq_len):
    hd = pl.program_id(1)
    qi = pl.program_id(2)
    tq, tk, d = DIFF_TQ, DIFF_TK, DIFF_HEAD_DIM
    hw = 2 * d
    f32, bf16 = jnp.float32, jnp.bfloat16

    @pl.when(qi == 0)
    def _stage_keys_values():
        lane = lax.broadcasted_iota(jnp.int32, (tk, hw), 1)
        row = lax.broadcasted_iota(jnp.int32, (tk, hw), 0)
        for c in range(seq_len // tk):
            r0 = c * tk
            pos = row + r0
            hi = (pos >> 6).astype(f32)
            lo = (pos & 63).astype(f32)
            aux = jnp.where(lane < 3, hi,
                            jnp.where(lane < 6, lo,
                                      jnp.where(lane < 9, 1.0, 0.0)))
            kaug[c, :, 0:hw] = k_ref[r0:r0 + tk, :]
            kaug[c, :, hw:2 * hw] = aux.astype(bf16)
            vt[c, 0:hw, :] = v_ref[r0:r0 + tk, :].astype(f32).T.astype(bf16)
            vt[c, hw:DIFF_VROWS, :] = jnp.ones((DIFF_VROWS - hw, tk), bf16)

    q_t = (q_ref[...].astype(f32) * (LOG2E * d ** -0.5)).T
    qrow = lax.broadcasted_iota(jnp.int32, (hw, tq), 0)
    qaug_t[0:hw, 0:tq] = jnp.where(qrow < d, q_t, 0.0).astype(bf16)
    qaug_t[0:hw, tq:2 * tq] = jnp.where(qrow >= d, q_t, 0.0).astype(bf16)
    a_coef = slopes_ref[hd] * LOG2E
    c_coef = -a_coef * (qi * tq).astype(f32)
    r = lax.broadcasted_iota(jnp.int32, (16, 2 * tq), 0)
    t = jnp.where(r < 3, a_coef * 64.0,
                  jnp.where(r < 6, a_coef, jnp.where(r < 9, c_coef, 0.0)))
    t0 = t.astype(bf16).astype(f32)
    t1 = (t - t0).astype(bf16).astype(f32)
    t2 = t - t0 - t1
    first = (r == 0) | (r == 3) | (r == 6)
    second = (r == 1) | (r == 4) | (r == 7)
    qaug_t[hw:hw + 16, :] = jnp.where(first, t0,
                                      jnp.where(second, t1, t2)).astype(bf16)
    qaug_t[hw + 16:2 * hw, :] = jnp.zeros((hw - 16, 2 * tq), bf16)

    acc_scr[...] = jnp.zeros(acc_scr.shape, f32)

    m_scr[...] = jnp.full(m_scr.shape, NEG_BIG, f32)

    def scores(j, dst):
        dst[...] = jnp.dot(kaug[j], qaug_t[...], preferred_element_type=f32)

    def softmax_pv(j, src, masked):
        s = src[...]
        if masked:
            key = lax.broadcasted_iota(jnp.int32, (tk, 2 * tq), 0)
            qry = lax.broadcasted_iota(jnp.int32, (tk, 2 * tq), 1)
            qry = jnp.where(qry >= tq, qry - tq, qry)
            s = jnp.where(key <= qry, s, NEG_BIG)
        m_old = m_scr[...]
        m_new = jnp.maximum(m_old, jnp.max(s, axis=0, keepdims=True))
        alpha = jnp.exp2(m_old - m_new)
        p = jnp.exp2(s - m_new).astype(bf16)
        acc_scr[...] = acc_scr[...] * alpha + jnp.dot(
            vt[j], p, preferred_element_type=f32)
        m_scr[...] = m_new

    def stage(j, src, dst):
        scores(j + 1, dst)
        softmax_pv(j, src, masked=False)

    scores(0, s_a)

    def pair(i, carry):
        stage(2 * i, s_a, s_b)
        stage(2 * i + 1, s_b, s_a)
        return carry

    lax.fori_loop(0, qi // 2, pair, 0)

    @pl.when(qi % 2 == 1)
    def _odd_tail():
        stage(qi - 1, s_a, s_b)
        softmax_pv(qi, s_b, masked=True)

    @pl.when(qi % 2 == 0)
    def _even_tail():
        softmax_pv(qi, s_a, masked=True)

    lam = (jnp.exp(jnp.sum(lamv_ref[0:1, :] * lamv_ref[1:2, :]))
           - jnp.exp(jnp.sum(lamv_ref[2:3, :] * lamv_ref[3:4, :]))
           + lambda_init)
    acc = acc_scr[...]
    o_all = acc[0:hw, :] * (1.0 / acc[hw:hw + 1, :])
    o_t = o_all[:, 0:tq] - lam * o_all[:, tq:2 * tq]
    o = _rms(o_t.T, subg_ref[...]) * (1.0 - lambda_init)
    o_ref[...] = o.astype(o_ref.dtype)


def diff_attention(qkv, slopes, lam_vecs, subln_g, lambda_init, batch, seq_len):
    m = qkv.shape[0]
    hw = 2 * DIFF_HEAD_DIM
    nq = seq_len // DIFF_TQ
    smem = pl.BlockSpec(memory_space=pltpu.SMEM)
    assert DIFF_TQ == DIFF_TK and seq_len % DIFF_TK == 0
    nk = seq_len // DIFF_TK
    return pl.pallas_call(
        functools.partial(_diff_kernel, lambda_init=lambda_init,
                          seq_len=seq_len),
        grid=(batch, DIFF_HEADS, nq),
        in_specs=[smem,
                  _const_spec(lam_vecs.shape),
                  _const_spec((1, hw)),
                  pl.BlockSpec((DIFF_TQ, hw), lambda b, h, i: (b * nq + i, h)),
                  pl.BlockSpec((seq_len, hw), lambda b, h, i: (b, DIFF_HEADS + h)),
                  pl.BlockSpec((seq_len, hw),
                               lambda b, h, i: (b, 2 * DIFF_HEADS + h))],
        out_specs=pl.BlockSpec((DIFF_TQ, hw), lambda b, h, i: (b * nq + i, h)),
        out_shape=jax.ShapeDtypeStruct((m, DIFF_HEADS * hw), jnp.bfloat16),
        scratch_shapes=[pltpu.VMEM((nk, DIFF_TK, 2 * hw), jnp.bfloat16),
                        pltpu.VMEM((nk, DIFF_VROWS, DIFF_TK), jnp.bfloat16),
                        pltpu.VMEM((2 * hw, 2 * DIFF_TQ), jnp.bfloat16),
                        pltpu.VMEM((DIFF_VROWS, 2 * DIFF_TQ), jnp.float32),
                        pltpu.VMEM((1, 2 * DIFF_TQ), jnp.float32),
                        pltpu.VMEM((DIFF_TK, 2 * DIFF_TQ), jnp.float32),
                        pltpu.VMEM((DIFF_TK, 2 * DIFF_TQ), jnp.float32)],
        compiler_params=_params(("parallel", "parallel", "arbitrary")),
        name="diff_attention",
    )(slopes, lam_vecs, subln_g.reshape(1, hw), qkv, qkv, qkv)


def _alibi_slopes(n_heads):
    return jnp.exp2(-8.0 * (jnp.arange(n_heads, dtype=jnp.float32) + 1.0) / n_heads)


def _diff_lambda_init(layer_idx):
    return 0.8 - 0.6 * math.exp(-0.3 * layer_idx)


def kernel(x, mix_pre_g, mix_post_g, ffn_pre_g, ffn_post_g, swa_w_qkv, swa_sinks, swa_w_o, diff_w_qkv, diff_lam_q1, diff_lam_k1, diff_lam_q2, diff_lam_k2, diff_subln_g, diff_w_o, ffn_w_up, ffn_conv_w, ffn_conv_b, ffn_w_down):
    batch, seq_len, d = x.shape
    depth = mix_pre_g.shape[0]
    bf = jnp.bfloat16
    x2d = x.reshape(batch * seq_len, d)
    for i in range(depth):
        j = i // N_MIXERS
        if i % N_MIXERS == 0:
            qkv = norm_matmul(x2d, mix_pre_g[i], swa_w_qkv[j].astype(bf),
                              n_chunk=640)
            a = swa_attention(qkv, _alibi_slopes(SWA_HEADS),
                              swa_sinks[j].astype(jnp.float32), batch, seq_len)
            w_o = swa_w_o[j]
        else:
            qkv = norm_matmul(x2d, mix_pre_g[i], diff_w_qkv[j].astype(bf))
            lam_vecs = jnp.stack([diff_lam_q1[j], diff_lam_k1[j],
                                  diff_lam_q2[j], diff_lam_k2[j]]).astype(jnp.float32)
            a = diff_attention(qkv, _alibi_slopes(DIFF_HEADS), lam_vecs,
                               diff_subln_g[j], _diff_lambda_init(i),
                               batch, seq_len)
            w_o = diff_w_o[j]
        x2d = matmul_norm_residual(a, w_o.astype(bf), mix_post_g[i], x2d)
        act = ffn_up(x2d, ffn_pre_g[i], ffn_w_up[i].astype(bf), ffn_conv_w[i],
                     ffn_conv_b[i], seq_len)
        x2d = matmul_norm_residual(act, ffn_w_down[i].astype(bf), ffn_post_g[i], x2d)
    return x2d.reshape(batch, seq_len, d)
```

```python
import functools
import math

import jax
import jax.numpy as jnp
from jax import lax
from jax.experimental import pallas as pl
from jax.experimental.pallas import tpu as pltpu

D_MODEL = 1024
EPS = 1e-6
BLOCK = 128
SWA_HEADS = 16
SWA_KV_HEADS = 2
SWA_HEAD_DIM = 64
SWA_GROUP = SWA_HEADS // SWA_KV_HEADS
DIFF_HEADS = 8
DIFF_HEAD_DIM = 64
D_FF = 2816
CONV_WIDTH = 3
N_MIXERS = 2

VMEM_LIMIT_BYTES = 56 * 1024 * 1024
LANES = 128
HALO = 16
GELU_C0 = math.sqrt(2.0 / math.pi)
GELU_C1 = 0.044715 * GELU_C0
NEG_BIG = -1e30

ROW_TILE = 1024
FFN_ROW_TILE = 1024
FF_CHUNK = 256
SWA_TILE = 512
SWA_VROWS = 80
DIFF_TQ = 512
DIFF_TK = 512
DIFF_VROWS = 144
LOG2E = 1.4426950408889634


def _rms(x, g):
    return x * lax.rsqrt(jnp.mean(x * x, axis=-1, keepdims=True) + EPS) * g


def _params(sem):
    return pltpu.CompilerParams(dimension_semantics=sem,
                                vmem_limit_bytes=VMEM_LIMIT_BYTES)


def _const_spec(shape):
    nd = len(shape)
    return pl.BlockSpec(shape, lambda *_: (0,) * nd,
                        pipeline_mode=pl.Buffered(1))


def _norm_matmul_kernel(x_ref, g_ref, w_ref, o_ref, *, n_chunk):
    h = _rms(x_ref[...], g_ref[...]).astype(jnp.bfloat16)
    n = o_ref.shape[1]
    for c in range(0, n, n_chunk):
        o_ref[:, c:c + n_chunk] = jnp.dot(
            h, w_ref[:, c:c + n_chunk],
            preferred_element_type=jnp.float32).astype(o_ref.dtype)


def norm_matmul(x2d, g, w_bf16, n_chunk=512):
    m, d = x2d.shape
    n = w_bf16.shape[1]
    return pl.pallas_call(
        functools.partial(_norm_matmul_kernel, n_chunk=n_chunk),
        grid=(m // ROW_TILE,),
        in_specs=[pl.BlockSpec((ROW_TILE, d), lambda i: (i, 0)),
                  _const_spec((1, d)),
                  _const_spec((d, n))],
        out_specs=pl.BlockSpec((ROW_TILE, n), lambda i: (i, 0)),
        out_shape=jax.ShapeDtypeStruct((m, n), jnp.bfloat16),
        compiler_params=_params(("parallel",)),
        name="norm_matmul",
    )(x2d, g.reshape(1, d), w_bf16)


def _matmul_norm_res_kernel(a_ref, w_ref, g_ref, x_ref, o_ref):
    y = jnp.dot(a_ref[...], w_ref[...], preferred_element_type=jnp.float32)
    o_ref[...] = x_ref[...] + _rms(y, g_ref[...])


def matmul_norm_residual(a_bf16, w_bf16, g, x2d):
    m, k = a_bf16.shape
    d = w_bf16.shape[1]
    return pl.pallas_call(
        _matmul_norm_res_kernel,
        grid=(m // ROW_TILE,),
        in_specs=[pl.BlockSpec((ROW_TILE, k), lambda i: (i, 0)),
                  _const_spec((k, d)),
                  _const_spec((1, d)),
                  pl.BlockSpec((ROW_TILE, d), lambda i: (i, 0))],
        out_specs=pl.BlockSpec((ROW_TILE, d), lambda i: (i, 0)),
        out_shape=jax.ShapeDtypeStruct((m, d), jnp.float32),
        compiler_params=_params(("parallel",)),
        name="matmul_norm_residual",
    )(a_bf16, w_bf16, g.reshape(1, d), x2d)


def _ffn_up_kernel(ahalo_ref, a_ref, wo_ref, gpost_ref, halo_ref, x_ref, g_ref,
                   wup_ref, cw_ref, cb_ref, x1_ref, o_ref,
                   h_scr, ug_scr, uv_scr, *, seq_len):
    i = pl.program_id(0)
    tm = x_ref.shape[0]
    a_ext = jnp.concatenate([ahalo_ref[...], a_ref[...]], axis=0)
    y = jnp.dot(a_ext, wo_ref[...], preferred_element_type=jnp.float32)
    x_ext = jnp.concatenate([halo_ref[...], x_ref[...]], axis=0)
    x1 = x_ext + _rms(y, gpost_ref[...])
    x1_ref[...] = x1[HALO:, :]
    g = g_ref[...]
    first = (i * tm) % seq_len == 0
    h_scr[0:HALO, :] = _rms(jnp.where(first, 0.0, x1[0:HALO, :]),
                            g).astype(jnp.bfloat16)
    h_scr[HALO:, :] = _rms(x1[HALO:, :], g).astype(jnp.bfloat16)

    def conv(u_scr, slab, col):
        c = cb_ref[:, col:col + LANES]
        for tap in range(CONV_WIDTH):
            off = HALO - (CONV_WIDTH - 1) + tap
            rows = pl.ds(off, tm) if off % 8 == 0 else pl.ds(off, tm, stride=1)
            c = c + cw_ref[tap:tap + 1, col:col + LANES] * u_scr[slab, rows, :]
        return c

    slabs = FF_CHUNK // LANES
    for j in range(D_FF // FF_CHUNK):
        cg = j * FF_CHUNK
        cv = D_FF + cg
        ug = jnp.dot(h_scr[...], wup_ref[:, cg:cg + FF_CHUNK],
                     preferred_element_type=jnp.float32)
        uv = jnp.dot(h_scr[...], wup_ref[:, cv:cv + FF_CHUNK],
                     preferred_element_type=jnp.float32)
        base = (j % 2) * slabs
        for s in range(slabs):
            ug_scr[base + s] = ug[:, s * LANES:(s + 1) * LANES]
            uv_scr[base + s] = uv[:, s * LANES:(s + 1) * LANES]
        for s in range(slabs):
            gate = conv(ug_scr, base + s, cg + s * LANES)
            val = conv(uv_scr, base + s, cv + s * LANES)
            t = jnp.tanh(gate * (GELU_C0 + GELU_C1 * (gate * gate)))
            o_ref[:, cg + s * LANES:cg + (s + 1) * LANES] = (
                (gate * val) * (0.5 * t + 0.5)).astype(o_ref.dtype)


def mixer_out_ffn_up(a_bf16, wo_bf16, g_post, x2d, g, wup_bf16, conv_w, conv_b,
                     seq_len):
    m, d = x2d.shape
    ka = a_bf16.shape[1]
    n2 = wup_bf16.shape[1]
    tm = FFN_ROW_TILE
    halo_blocks = tm // HALO
    u_scratch = pltpu.VMEM((2 * FF_CHUNK // LANES, tm + HALO, LANES), jnp.float32)

    def halo_map(i):
        return (jnp.maximum(i * halo_blocks - 1, 0), 0)

    def tile_map(i):
        return (i, 0)

    return pl.pallas_call(
        functools.partial(_ffn_up_kernel, seq_len=seq_len),
        grid=(m // tm,),
        in_specs=[pl.BlockSpec((HALO, ka), halo_map),
                  pl.BlockSpec((tm, ka), tile_map),
                  _const_spec((ka, d)),
                  _const_spec((1, d)),
                  pl.BlockSpec((HALO, d), halo_map),
                  pl.BlockSpec((tm, d), tile_map),
                  _const_spec((1, d)),
                  _const_spec((d, n2)),
                  _const_spec((CONV_WIDTH, n2)),
                  _const_spec((1, n2))],
        out_specs=[pl.BlockSpec((tm, d), tile_map),
                   pl.BlockSpec((tm, D_FF), tile_map)],
        out_shape=[jax.ShapeDtypeStruct((m, d), jnp.float32),
                   jax.ShapeDtypeStruct((m, D_FF), jnp.bfloat16)],
        scratch_shapes=[pltpu.VMEM((tm + HALO, d), jnp.bfloat16),
                        u_scratch, u_scratch],
        compiler_params=_params(("parallel",)),
        name="mixer_out_ffn_up",
    )(a_bf16, a_bf16, wo_bf16, g_post.reshape(1, d), x2d, x2d, g.reshape(1, d),
      wup_bf16, conv_w, conv_b.reshape(1, n2))


def _swa_kernel(slopes_ref, sinks_ref, q_ref, k_ref, kp_ref, v_ref, vp_ref,
                o_ref, qt_all, k_aug, vt_aug, p_scr):
    t = pl.program_id(1)
    f32, bf16 = jnp.float32, jnp.bfloat16
    d, grp = SWA_HEAD_DIM, SWA_GROUP
    ncol = SWA_HEADS * BLOCK
    kvw = SWA_KV_HEADS * d

    @pl.when((pl.program_id(0) == 0) & (t == 0))
    def _init_constants():
        colh = lax.broadcasted_iota(jnp.int32, (16, ncol), 1) >> 7
        r = lax.broadcasted_iota(jnp.int32, (16, ncol), 0)
        a = jnp.zeros((16, ncol), f32)
        for h in range(SWA_HEADS):
            a = jnp.where(colh == h, slopes_ref[h] * LOG2E, a)
        a0 = a.astype(bf16).astype(f32)
        a1 = (a - a0).astype(bf16).astype(f32)
        a2 = a - a0 - a1
        coef = jnp.where(r == 0, a0, jnp.where(r == 1, a1,
                                               jnp.where(r == 2, a2, 0.0)))
        kj = lax.broadcasted_iota(jnp.int32, (2 * BLOCK, kvw), 0).astype(f32)
        lane = lax.broadcasted_iota(jnp.int32, (2 * BLOCK, kvw), 1)
        for par in range(SWA_TILE // BLOCK):
            qt_all[par] = jnp.zeros((2 * kvw, ncol), bf16)
            qt_all[par, kvw:kvw + 16, :] = coef.astype(bf16)
            k_aug[par, :, kvw:2 * kvw] = jnp.where(lane < 3, kj, 0.0).astype(bf16)
            for kv in range(SWA_KV_HEADS):
                vt_aug[par, kv, d:SWA_VROWS, :] = jnp.ones(
                    (SWA_VROWS - d, 2 * BLOCK), bf16)

    key = lax.broadcasted_iota(jnp.int32, (2 * BLOCK, BLOCK), 0)
    qry = lax.broadcasted_iota(jnp.int32, (2 * BLOCK, BLOCK), 1)
    dist = qry + BLOCK - key
    band = (dist >= 0) & (dist < BLOCK)
    band_first = band & (key >= jnp.where(t > 0, 0, BLOCK))
    qpos = (lax.broadcasted_iota(jnp.int32, (1, BLOCK), 1) + BLOCK).astype(f32)

    for blk in range(SWA_TILE // BLOCK):
        par = blk
        r0 = blk * BLOCK
        valid = band_first if blk == 0 else band
        k_prev = kp_ref[...] if blk == 0 else k_ref[r0 - BLOCK:r0, :]
        v_prev = vp_ref[...] if blk == 0 else v_ref[r0 - BLOCK:r0, :]
        k_aug[par, 0:BLOCK, 0:kvw] = k_prev
        k_aug[par, BLOCK:2 * BLOCK, 0:kvw] = k_ref[r0:r0 + BLOCK, :]
        v_t = jnp.concatenate([v_prev, v_ref[r0:r0 + BLOCK, :]],
                              axis=0).astype(f32).T
        for kv in range(SWA_KV_HEADS):
            vt_aug[par, kv, 0:d, :] = v_t[kv * d:(kv + 1) * d, :].astype(bf16)
        q_t = (q_ref[r0:r0 + BLOCK, :].astype(f32) * (LOG2E * d ** -0.5)).T
        for h in range(SWA_HEADS):
            kv = h // grp
            qt_all[par, kv * d:(kv + 1) * d, h * BLOCK:(h + 1) * BLOCK] = (
                q_t[h * d:(h + 1) * d, :].astype(bf16))

        s = jnp.dot(k_aug[par], qt_all[par], preferred_element_type=f32)
        m_list, sink_list = [], []
        for h in range(SWA_HEADS):
            s_h = jnp.where(valid, s[:, h * BLOCK:(h + 1) * BLOCK], NEG_BIG)
            sink_h = (sinks_ref[h] * LOG2E
                      + (slopes_ref[h] * LOG2E) * qpos)
            m_h = jnp.maximum(jnp.max(s_h, axis=0, keepdims=True), sink_h)
            p_scr[par, :, h * BLOCK:(h + 1) * BLOCK] = (
                jnp.exp2(s_h - m_h).astype(bf16))
            m_list.append(m_h)
            sink_list.append(sink_h)

        for kv in range(SWA_KV_HEADS):
            pv = jnp.dot(vt_aug[par, kv],
                         p_scr[par, :, kv * grp * BLOCK:(kv + 1) * grp * BLOCK],
                         preferred_element_type=f32)
            for pair in range(grp // 2):
                halves = []
                for g in (2 * pair, 2 * pair + 1):
                    h = kv * grp + g
                    cols = slice(g * BLOCK, (g + 1) * BLOCK)
                    denom = pv[d:d + 1, cols] + jnp.exp2(
                        sink_list[h] - m_list[h])
                    halves.append(pv[0:d, cols] / denom)
                o_pair = jnp.concatenate(halves, axis=0).T
                c0 = (kv * grp + 2 * pair) * d
                o_ref[r0:r0 + BLOCK, c0:c0 + 2 * d] = o_pair.astype(o_ref.dtype)


def swa_attention(qkv, slopes, sinks, batch, seq_len):
    m = qkv.shape[0]
    nt = seq_len // SWA_TILE
    per_tile = SWA_TILE // BLOCK
    hq = SWA_HEADS * SWA_HEAD_DIM
    kv_w = SWA_KV_HEADS * SWA_HEAD_DIM
    k_col = hq // kv_w
    v_col = k_col + 1
    ncol = SWA_HEADS * BLOCK

    def cur(col):
        return lambda b, t: (b * nt + t, col)

    def prev(col):
        return lambda b, t: (jnp.maximum((b * nt + t) * per_tile - 1, 0), col)

    smem = pl.BlockSpec(memory_space=pltpu.SMEM)
    return pl.pallas_call(
        _swa_kernel,
        grid=(batch, nt),
        in_specs=[smem, smem,
                  pl.BlockSpec((SWA_TILE, hq), cur(0)),
                  pl.BlockSpec((SWA_TILE, kv_w), cur(k_col)),
                  pl.BlockSpec((BLOCK, kv_w), prev(k_col)),
                  pl.BlockSpec((SWA_TILE, kv_w), cur(v_col)),
                  pl.BlockSpec((BLOCK, kv_w), prev(v_col))],
        out_specs=pl.BlockSpec((SWA_TILE, hq), cur(0)),
        out_shape=jax.ShapeDtypeStruct((m, hq), jnp.bfloat16),
        scratch_shapes=[pltpu.VMEM((per_tile, 2 * kv_w, ncol), jnp.bfloat16),
                        pltpu.VMEM((per_tile, 2 * BLOCK, 2 * kv_w), jnp.bfloat16),
                        pltpu.VMEM((per_tile, SWA_KV_HEADS, SWA_VROWS, 2 * BLOCK),
                                   jnp.bfloat16),
                        pltpu.VMEM((per_tile, 2 * BLOCK, ncol), jnp.bfloat16)],
        compiler_params=_params(("arbitrary", "arbitrary")),
        name="swa_attention",
    )(slopes, sinks, qkv, qkv, qkv, qkv, qkv)


def _diff_kernel(slopes_ref, lamv_ref, subg_ref, q_ref, k_ref, v_ref, o_ref,
                 kaug, vt, qaug_t, acc_scr, m_scr, s_a, s_b, s_c, *,
                 lambda_init, seq_len):
    hd = pl.program_id(1)
    qi = pl.program_id(2)
    tq, tk, d = DIFF_TQ, DIFF_TK, DIFF_HEAD_DIM
    hw = 2 * d
    f32, bf16 = jnp.float32, jnp.bfloat16

    @pl.when(qi == 0)
    def _stage_keys_values():
        lane = lax.broadcasted_iota(jnp.int32, (tk, hw), 1)
        row = lax.broadcasted_iota(jnp.int32, (tk, hw), 0)
        for c in range(seq_len // tk):
            r0 = c * tk
            pos = row + r0
            hi = (pos >> 6).astype(f32)
            lo = (pos & 63).astype(f32)
            aux = jnp.where(lane < 3, hi,
                            jnp.where(lane < 6, lo,
                                      jnp.where(lane < 9, 1.0, 0.0)))
            kaug[c, :, 0:hw] = k_ref[r0:r0 + tk, :]
            kaug[c, :, hw:2 * hw] = aux.astype(bf16)
            vt[c, 0:hw, :] = v_ref[r0:r0 + tk, :].astype(f32).T.astype(bf16)
            vt[c, hw:DIFF_VROWS, :] = jnp.ones((DIFF_VROWS - hw, tk), bf16)

    def setup_queries(tile):
        q0 = pl.multiple_of(tile * tq, tq)
        q_t = (q_ref[pl.ds(q0, tq), :].astype(f32) * (LOG2E * d ** -0.5)).T
        qrow = lax.broadcasted_iota(jnp.int32, (hw, tq), 0)
        qaug_t[0:hw, 0:tq] = jnp.where(qrow < d, q_t, 0.0).astype(bf16)
        qaug_t[0:hw, tq:2 * tq] = jnp.where(qrow >= d, q_t, 0.0).astype(bf16)
        a_coef = slopes_ref[hd] * LOG2E
        c_coef = -a_coef * q0.astype(f32)
        r = lax.broadcasted_iota(jnp.int32, (16, 2 * tq), 0)
        t = jnp.where(r < 3, a_coef * 64.0,
                      jnp.where(r < 6, a_coef, jnp.where(r < 9, c_coef, 0.0)))
        t0 = t.astype(bf16).astype(f32)
        t1 = (t - t0).astype(bf16).astype(f32)
        t2 = t - t0 - t1
        first = (r == 0) | (r == 3) | (r == 6)
        second = (r == 1) | (r == 4) | (r == 7)
        qaug_t[hw:hw + 16, :] = jnp.where(
            first, t0, jnp.where(second, t1, t2)).astype(bf16)
        qaug_t[hw + 16:2 * hw, :] = jnp.zeros((hw - 16, 2 * tq), bf16)

    def scores(j, dst):
        dst[...] = jnp.dot(kaug[j], qaug_t[...], preferred_element_type=f32)

    def softmax_pv(j, src, masked):
        s = src[...]
        if masked:
            key = lax.broadcasted_iota(jnp.int32, (tk, 2 * tq), 0)
            qry = lax.broadcasted_iota(jnp.int32, (tk, 2 * tq), 1)
            qry = jnp.where(qry >= tq, qry - tq, qry)
            s = jnp.where(key <= qry, s, NEG_BIG)
        m_old = m_scr[...]
        m_new = jnp.maximum(m_old, jnp.max(s, axis=0, keepdims=True))
        alpha = jnp.exp2(m_old - m_new)
        p = jnp.exp2(s - m_new).astype(bf16)
        acc_scr[...] = acc_scr[...] * alpha + jnp.dot(
            vt[j], p, preferred_element_type=f32)
        m_scr[...] = m_new

    def normalise_and_store():
        lam = (jnp.exp(jnp.sum(lamv_ref[0:1, :] * lamv_ref[1:2, :]))
               - jnp.exp(jnp.sum(lamv_ref[2:3, :] * lamv_ref[3:4, :]))
               + lambda_init)
        acc = acc_scr[...]
        o_all = acc[0:hw, :] * (1.0 / acc[hw:hw + 1, :])
        o_t = o_all[:, 0:tq] - lam * o_all[:, tq:2 * tq]
        o_t = o_t * lax.rsqrt(jnp.mean(o_t * o_t, axis=0, keepdims=True) + EPS)
        o = o_t.T * (subg_ref[...] * (1.0 - lambda_init))
        o_ref[pl.ds(pl.multiple_of(qi * tq, tq), tq), :] = o.astype(o_ref.dtype)

    def stage(j, src, dst):
        scores(j + 1, dst)
        softmax_pv(j, src, masked=False)

    def finish(src):
        softmax_pv(qi, src, masked=True)
        setup_queries(jnp.minimum(qi + 1, seq_len // tq - 1))
        scores(0, s_c)
        normalise_and_store()

    @pl.when(qi == 0)
    def _first_scores():
        setup_queries(0)
        scores(0, s_c)

    acc_scr[...] = jnp.zeros(acc_scr.shape, f32)
    m_scr[...] = jnp.full(m_scr.shape, NEG_BIG, f32)

    @pl.when(qi >= 1)
    def _stage0():
        stage(0, s_c, s_a)

    def pair(i, carry):
        stage(2 * i + 1, s_a, s_b)
        stage(2 * i + 2, s_b, s_a)
        return carry

    lax.fori_loop(0, jnp.maximum(qi - 1, 0) // 2, pair, 0)

    even_tail = (qi % 2 == 0) & (qi >= 2)

    @pl.when(even_tail)
    def _last_unmasked():
        stage(qi - 1, s_a, s_b)

    @pl.when(even_tail)
    def _finish_even():
        finish(s_b)

    @pl.when(qi % 2 == 1)
    def _finish_odd():
        finish(s_a)

    @pl.when(qi == 0)
    def _finish_first():
        finish(s_c)


def diff_attention(qkv, slopes, lam_vecs, subln_g, lambda_init, batch, seq_len):
    m = qkv.shape[0]
    hw = 2 * DIFF_HEAD_DIM
    nq = seq_len // DIFF_TQ
    smem = pl.BlockSpec(memory_space=pltpu.SMEM)
    assert DIFF_TQ == DIFF_TK and seq_len % DIFF_TK == 0
    nk = seq_len // DIFF_TK
    score_buf = pltpu.VMEM((DIFF_TK, 2 * DIFF_TQ), jnp.float32)
    return pl.pallas_call(
        functools.partial(_diff_kernel, lambda_init=lambda_init,
                          seq_len=seq_len),
        grid=(batch, DIFF_HEADS, nq),
        in_specs=[smem,
                  _const_spec(lam_vecs.shape),
                  _const_spec((1, hw)),
                  pl.BlockSpec((seq_len, hw), lambda b, h, i: (b, h)),
                  pl.BlockSpec((seq_len, hw), lambda b, h, i: (b, DIFF_HEADS + h)),
                  pl.BlockSpec((seq_len, hw),
                               lambda b, h, i: (b, 2 * DIFF_HEADS + h))],
        out_specs=pl.BlockSpec((seq_len, hw), lambda b, h, i: (b, h)),
        out_shape=jax.ShapeDtypeStruct((m, DIFF_HEADS * hw), jnp.bfloat16),
        scratch_shapes=[pltpu.VMEM((nk, DIFF_TK, 2 * hw), jnp.bfloat16),
                        pltpu.VMEM((nk, DIFF_VROWS, DIFF_TK), jnp.bfloat16),
                        pltpu.VMEM((2 * hw, 2 * DIFF_TQ), jnp.bfloat16),
                        pltpu.VMEM((DIFF_VROWS, 2 * DIFF_TQ), jnp.float32),
                        pltpu.VMEM((1, 2 * DIFF_TQ), jnp.float32),
                        score_buf, score_buf, score_buf],
        compiler_params=_params(("parallel", "parallel", "arbitrary")),
        name="diff_attention",
    )(slopes, lam_vecs, subln_g.reshape(1, hw), qkv, qkv, qkv)


def _alibi_slopes(n_heads):
    return jnp.exp2(-8.0 * (jnp.arange(n_heads, dtype=jnp.float32) + 1.0) / n_heads)


def _diff_lambda_init(layer_idx):
    return 0.8 - 0.6 * math.exp(-0.3 * layer_idx)


def kernel(x, mix_pre_g, mix_post_g, ffn_pre_g, ffn_post_g, swa_w_qkv, swa_sinks, swa_w_o, diff_w_qkv, diff_lam_q1, diff_lam_k1, diff_lam_q2, diff_lam_k2, diff_subln_g, diff_w_o, ffn_w_up, ffn_conv_w, ffn_conv_b, ffn_w_down):
    batch, seq_len, d = x.shape
    depth = mix_pre_g.shape[0]
    bf = jnp.bfloat16
    x2d = x.reshape(batch * seq_len, d)
    for i in range(depth):
        j = i // N_MIXERS
        if i % N_MIXERS == 0:
            qkv = norm_matmul(x2d, mix_pre_g[i], swa_w_qkv[j].astype(bf),
                              n_chunk=640)
            a = swa_attention(qkv, _alibi_slopes(SWA_HEADS),
                              swa_sinks[j].astype(jnp.float32), batch, seq_len)
            w_o = swa_w_o[j]
        else:
            qkv = norm_matmul(x2d, mix_pre_g[i], diff_w_qkv[j].astype(bf))
            lam_vecs = jnp.stack([diff_lam_q1[j], diff_lam_k1[j],
                                  diff_lam_q2[j], diff_lam_k2[j]]).astype(jnp.float32)
            a = diff_attention(qkv, _alibi_slopes(DIFF_HEADS), lam_vecs,
                               diff_subln_g[j], _diff_lambda_init(i),
                               batch, seq_len)
            w_o = diff_w_o[j]
        x2d, act = mixer_out_ffn_up(a, w_o.astype(bf), mix_post_g[i], x2d,
                                    ffn_pre_g[i], ffn_w_up[i].astype(bf),
                                    ffn_conv_w[i], ffn_conv_b[i], seq_len)
        x2d = matmul_norm_residual(act, ffn_w_down[i].astype(bf), ffn_post_g[i], x2d)
    return x2d.reshape(batch, seq_len, d)
```

```python
import functools
import math

import jax
import jax.numpy as jnp
from jax import lax
from jax.experimental import pallas as pl
from jax.experimental.pallas import tpu as pltpu

D_MODEL = 1024
EPS = 1e-6
BLOCK = 128
SWA_HEADS = 16
SWA_KV_HEADS = 2
SWA_HEAD_DIM = 64
SWA_GROUP = SWA_HEADS // SWA_KV_HEADS
DIFF_HEADS = 8
DIFF_HEAD_DIM = 64
D_FF = 2816
CONV_WIDTH = 3
N_MIXERS = 2

VMEM_LIMIT_BYTES = 56 * 1024 * 1024
LANES = 128
HALO = 16
GELU_C0 = math.sqrt(2.0 / math.pi)
GELU_C1 = 0.044715 * GELU_C0
NEG_BIG = -1e30

ROW_TILE = 1024
FFN_ROW_TILE = 1024
FF_CHUNK = 256
SWA_TILE = 512
SWA_VROWS = 80
DIFF_TQ = 512
DIFF_TK = 512
DIFF_VROWS = 144
DIFF_HEADS_PER_STEP = 1
LOG2E = 1.4426950408889634


def _rms(x, g):
    return x * lax.rsqrt(jnp.mean(x * x, axis=-1, keepdims=True) + EPS) * g


def _params(sem):
    return pltpu.CompilerParams(dimension_semantics=sem,
                                vmem_limit_bytes=VMEM_LIMIT_BYTES)


def _const_spec(shape):
    nd = len(shape)
    return pl.BlockSpec(shape, lambda *_: (0,) * nd,
                        pipeline_mode=pl.Buffered(1))


def _norm_matmul_kernel(x_ref, g_ref, w_ref, o_ref, *, n_chunk):
    h = _rms(x_ref[...], g_ref[...]).astype(jnp.bfloat16)
    n = o_ref.shape[1]
    for c in range(0, n, n_chunk):
        o_ref[:, c:c + n_chunk] = jnp.dot(
            h, w_ref[:, c:c + n_chunk],
            preferred_element_type=jnp.float32).astype(o_ref.dtype)


def norm_matmul(x2d, g, w_bf16, n_chunk=512):
    m, d = x2d.shape
    n = w_bf16.shape[1]
    return pl.pallas_call(
        functools.partial(_norm_matmul_kernel, n_chunk=n_chunk),
        grid=(m // ROW_TILE,),
        in_specs=[pl.BlockSpec((ROW_TILE, d), lambda i: (i, 0)),
                  _const_spec((1, d)),
                  _const_spec((d, n))],
        out_specs=pl.BlockSpec((ROW_TILE, n), lambda i: (i, 0)),
        out_shape=jax.ShapeDtypeStruct((m, n), jnp.bfloat16),
        compiler_params=_params(("parallel",)),
        name="norm_matmul",
    )(x2d, g.reshape(1, d), w_bf16)


def _matmul_norm_res_kernel(a_ref, w_ref, g_ref, x_ref, o_ref):
    y = jnp.dot(a_ref[...], w_ref[...], preferred_element_type=jnp.float32)
    o_ref[...] = x_ref[...] + _rms(y, g_ref[...])


def matmul_norm_residual(a_bf16, w_bf16, g, x2d):
    m, k = a_bf16.shape
    d = w_bf16.shape[1]
    return pl.pallas_call(
        _matmul_norm_res_kernel,
        grid=(m // ROW_TILE,),
        in_specs=[pl.BlockSpec((ROW_TILE, k), lambda i: (i, 0)),
                  _const_spec((k, d)),
                  _const_spec((1, d)),
                  pl.BlockSpec((ROW_TILE, d), lambda i: (i, 0))],
        out_specs=pl.BlockSpec((ROW_TILE, d), lambda i: (i, 0)),
        out_shape=jax.ShapeDtypeStruct((m, d), jnp.float32),
        compiler_params=_params(("parallel",)),
        name="matmul_norm_residual",
    )(a_bf16, w_bf16, g.reshape(1, d), x2d)


def _ffn_up_kernel(ahalo_ref, a_ref, wo_ref, gpost_ref, halo_ref, x_ref, g_ref,
                   wup_ref, cw_ref, cb_ref, x1_ref, o_ref,
                   h_scr, ug_scr, uv_scr, *, seq_len):
    i = pl.program_id(0)
    tm = x_ref.shape[0]
    a_ext = jnp.concatenate([ahalo_ref[...], a_ref[...]], axis=0)
    y = jnp.dot(a_ext, wo_ref[...], preferred_element_type=jnp.float32)
    x_ext = jnp.concatenate([halo_ref[...], x_ref[...]], axis=0)
    x1 = x_ext + _rms(y, gpost_ref[...])
    x1_ref[...] = x1[HALO:, :]
    g = g_ref[...]
    first = (i * tm) % seq_len == 0
    h_scr[0:HALO, :] = _rms(jnp.where(first, 0.0, x1[0:HALO, :]),
                            g).astype(jnp.bfloat16)
    h_scr[HALO:, :] = _rms(x1[HALO:, :], g).astype(jnp.bfloat16)

    def conv(u_scr, slab, col):
        c = cb_ref[:, col:col + LANES]
        for tap in range(CONV_WIDTH):
            off = HALO - (CONV_WIDTH - 1) + tap
            rows = pl.ds(off, tm) if off % 8 == 0 else pl.ds(off, tm, stride=1)
            c = c + cw_ref[tap:tap + 1, col:col + LANES] * u_scr[slab, rows, :]
        return c

    slabs = FF_CHUNK // LANES
    for j in range(D_FF // FF_CHUNK):
        cg = j * FF_CHUNK
        cv = D_FF + cg
        ug = jnp.dot(h_scr[...], wup_ref[:, cg:cg + FF_CHUNK],
                     preferred_element_type=jnp.float32)
        uv = jnp.dot(h_scr[...], wup_ref[:, cv:cv + FF_CHUNK],
                     preferred_element_type=jnp.float32)
        base = (j % 2) * slabs
        for s in range(slabs):
            ug_scr[base + s] = ug[:, s * LANES:(s + 1) * LANES]
            uv_scr[base + s] = uv[:, s * LANES:(s + 1) * LANES]
        for s in range(slabs):
            gate = conv(ug_scr, base + s, cg + s * LANES)
            val = conv(uv_scr, base + s, cv + s * LANES)
            t = jnp.tanh(gate * (GELU_C0 + GELU_C1 * (gate * gate)))
            o_ref[:, cg + s * LANES:cg + (s + 1) * LANES] = (
                (gate * val) * (0.5 * t + 0.5)).astype(o_ref.dtype)


def mixer_out_ffn_up(a_bf16, wo_bf16, g_post, x2d, g, wup_bf16, conv_w, conv_b,
                     seq_len):
    m, d = x2d.shape
    ka = a_bf16.shape[1]
    n2 = wup_bf16.shape[1]
    tm = FFN_ROW_TILE
    halo_blocks = tm // HALO
    u_scratch = pltpu.VMEM((2 * FF_CHUNK // LANES, tm + HALO, LANES), jnp.float32)

    def halo_map(i):
        return (jnp.maximum(i * halo_blocks - 1, 0), 0)

    def tile_map(i):
        return (i, 0)

    return pl.pallas_call(
        functools.partial(_ffn_up_kernel, seq_len=seq_len),
        grid=(m // tm,),
        in_specs=[pl.BlockSpec((HALO, ka), halo_map),
                  pl.BlockSpec((tm, ka), tile_map),
                  _const_spec((ka, d)),
                  _const_spec((1, d)),
                  pl.BlockSpec((HALO, d), halo_map),
                  pl.BlockSpec((tm, d), tile_map),
                  _const_spec((1, d)),
                  _const_spec((d, n2)),
                  _const_spec((CONV_WIDTH, n2)),
                  _const_spec((1, n2))],
        out_specs=[pl.BlockSpec((tm, d), tile_map),
                   pl.BlockSpec((tm, D_FF), tile_map)],
        out_shape=[jax.ShapeDtypeStruct((m, d), jnp.float32),
                   jax.ShapeDtypeStruct((m, D_FF), jnp.bfloat16)],
        scratch_shapes=[pltpu.VMEM((tm + HALO, d), jnp.bfloat16),
                        u_scratch, u_scratch],
        compiler_params=_params(("parallel",)),
        name="mixer_out_ffn_up",
    )(a_bf16, a_bf16, wo_bf16, g_post.reshape(1, d), x2d, x2d, g.reshape(1, d),
      wup_bf16, conv_w, conv_b.reshape(1, n2))


def _swa_kernel(slopes_ref, sinks_ref, q_ref, k_ref, kp_ref, v_ref, vp_ref,
                o_ref, qt_all, k_aug, vt_aug, p_scr):
    t = pl.program_id(1)
    f32, bf16 = jnp.float32, jnp.bfloat16
    d, grp = SWA_HEAD_DIM, SWA_GROUP
    ncol = SWA_HEADS * BLOCK
    kvw = SWA_KV_HEADS * d

    @pl.when((pl.program_id(0) == 0) & (t == 0))
    def _init_constants():
        colh = lax.broadcasted_iota(jnp.int32, (16, ncol), 1) >> 7
        r = lax.broadcasted_iota(jnp.int32, (16, ncol), 0)
        a = jnp.zeros((16, ncol), f32)
        for h in range(SWA_HEADS):
            a = jnp.where(colh == h, slopes_ref[h] * LOG2E, a)
        a0 = a.astype(bf16).astype(f32)
        a1 = (a - a0).astype(bf16).astype(f32)
        a2 = a - a0 - a1
        coef = jnp.where(r == 0, a0, jnp.where(r == 1, a1,
                                               jnp.where(r == 2, a2, 0.0)))
        kj = lax.broadcasted_iota(jnp.int32, (2 * BLOCK, kvw), 0).astype(f32)
        lane = lax.broadcasted_iota(jnp.int32, (2 * BLOCK, kvw), 1)
        for par in range(SWA_TILE // BLOCK):
            qt_all[par] = jnp.zeros((2 * kvw, ncol), bf16)
            qt_all[par, kvw:kvw + 16, :] = coef.astype(bf16)
            k_aug[par, :, kvw:2 * kvw] = jnp.where(lane < 3, kj, 0.0).astype(bf16)
            for kv in range(SWA_KV_HEADS):
                vt_aug[par, kv, d:SWA_VROWS, :] = jnp.ones(
                    (SWA_VROWS - d, 2 * BLOCK), bf16)

    key = lax.broadcasted_iota(jnp.int32, (2 * BLOCK, BLOCK), 0)
    qry = lax.broadcasted_iota(jnp.int32, (2 * BLOCK, BLOCK), 1)
    dist = qry + BLOCK - key
    band = (dist >= 0) & (dist < BLOCK)
    band_first = band & (key >= jnp.where(t > 0, 0, BLOCK))
    qpos = (lax.broadcasted_iota(jnp.int32, (1, BLOCK), 1) + BLOCK).astype(f32)

    for blk in range(SWA_TILE // BLOCK):
        par = blk
        r0 = blk * BLOCK
        valid = band_first if blk == 0 else band
        k_prev = kp_ref[...] if blk == 0 else k_ref[r0 - BLOCK:r0, :]
        v_prev = vp_ref[...] if blk == 0 else v_ref[r0 - BLOCK:r0, :]
        k_aug[par, 0:BLOCK, 0:kvw] = k_prev
        k_aug[par, BLOCK:2 * BLOCK, 0:kvw] = k_ref[r0:r0 + BLOCK, :]
        v_t = jnp.concatenate([v_prev, v_ref[r0:r0 + BLOCK, :]],
                              axis=0).astype(f32).T
        for kv in range(SWA_KV_HEADS):
            vt_aug[par, kv, 0:d, :] = v_t[kv * d:(kv + 1) * d, :].astype(bf16)
        q_t = (q_ref[r0:r0 + BLOCK, :].astype(f32) * (LOG2E * d ** -0.5)).T
        for h in range(SWA_HEADS):
            kv = h // grp
            qt_all[par, kv * d:(kv + 1) * d, h * BLOCK:(h + 1) * BLOCK] = (
                q_t[h * d:(h + 1) * d, :].astype(bf16))

        s = jnp.dot(k_aug[par], qt_all[par], preferred_element_type=f32)
        m_list, sink_list = [], []
        for h in range(SWA_HEADS):
            s_h = jnp.where(valid, s[:, h * BLOCK:(h + 1) * BLOCK], NEG_BIG)
            sink_h = (sinks_ref[h] * LOG2E
                      + (slopes_ref[h] * LOG2E) * qpos)
            m_h = jnp.maximum(jnp.max(s_h, axis=0, keepdims=True), sink_h)
            p_scr[par, :, h * BLOCK:(h + 1) * BLOCK] = (
                jnp.exp2(s_h - m_h).astype(bf16))
            m_list.append(m_h)
            sink_list.append(sink_h)

        for kv in range(SWA_KV_HEADS):
            pv = jnp.dot(vt_aug[par, kv],
                         p_scr[par, :, kv * grp * BLOCK:(kv + 1) * grp * BLOCK],
                         preferred_element_type=f32)
            for pair in range(grp // 2):
                halves = []
                for g in (2 * pair, 2 * pair + 1):
                    h = kv * grp + g
                    cols = slice(g * BLOCK, (g + 1) * BLOCK)
                    denom = pv[d:d + 1, cols] + jnp.exp2(
                        sink_list[h] - m_list[h])
                    halves.append(pv[0:d, cols] / denom)
                o_pair = jnp.concatenate(halves, axis=0).T
                c0 = (kv * grp + 2 * pair) * d
                o_ref[r0:r0 + BLOCK, c0:c0 + 2 * d] = o_pair.astype(o_ref.dtype)


def swa_attention(qkv, slopes, sinks, batch, seq_len):
    m = qkv.shape[0]
    nt = seq_len // SWA_TILE
    per_tile = SWA_TILE // BLOCK
    hq = SWA_HEADS * SWA_HEAD_DIM
    kv_w = SWA_KV_HEADS * SWA_HEAD_DIM
    k_col = hq // kv_w
    v_col = k_col + 1
    ncol = SWA_HEADS * BLOCK

    def cur(col):
        return lambda b, t: (b * nt + t, col)

    def prev(col):
        return lambda b, t: (jnp.maximum((b * nt + t) * per_tile - 1, 0), col)

    smem = pl.BlockSpec(memory_space=pltpu.SMEM)
    return pl.pallas_call(
        _swa_kernel,
        grid=(batch, nt),
        in_specs=[smem, smem,
                  pl.BlockSpec((SWA_TILE, hq), cur(0)),
                  pl.BlockSpec((SWA_TILE, kv_w), cur(k_col)),
                  pl.BlockSpec((BLOCK, kv_w), prev(k_col)),
                  pl.BlockSpec((SWA_TILE, kv_w), cur(v_col)),
                  pl.BlockSpec((BLOCK, kv_w), prev(v_col))],
        out_specs=pl.BlockSpec((SWA_TILE, hq), cur(0)),
        out_shape=jax.ShapeDtypeStruct((m, hq), jnp.bfloat16),
        scratch_shapes=[pltpu.VMEM((per_tile, 2 * kv_w, ncol), jnp.bfloat16),
                        pltpu.VMEM((per_tile, 2 * BLOCK, 2 * kv_w), jnp.bfloat16),
                        pltpu.VMEM((per_tile, SWA_KV_HEADS, SWA_VROWS, 2 * BLOCK),
                                   jnp.bfloat16),
                        pltpu.VMEM((per_tile, 2 * BLOCK, ncol), jnp.bfloat16)],
        compiler_params=_params(("arbitrary", "arbitrary")),
        name="swa_attention",
    )(slopes, sinks, qkv, qkv, qkv, qkv, qkv)


def _diff_kernel(slopes_ref, lamv_ref, subg_ref, q_ref, k_ref, v_ref, o_ref,
                 kaug, vt, qaug_t, acc_scr, m_scr, s_a, s_b, s_c, *,
                 lambda_init, seq_len):
    hp = pl.program_id(1)
    tq, tk, d = DIFF_TQ, DIFF_TK, DIFF_HEAD_DIM
    hw = 2 * d
    nq = seq_len // tq
    f32, bf16 = jnp.float32, jnp.bfloat16
    heads = range(DIFF_HEADS_PER_STEP)

    def stage_keys_values():
        lane = lax.broadcasted_iota(jnp.int32, (tk, hw), 1)
        row = lax.broadcasted_iota(jnp.int32, (tk, hw), 0)
        for c in range(seq_len // tk):
            r0 = c * tk
            pos = row + r0
            hi = (pos >> 6).astype(f32)
            lo = (pos & 63).astype(f32)
            aux = jnp.where(lane < 3, hi,
                            jnp.where(lane < 6, lo,
                                      jnp.where(lane < 9, 1.0, 0.0)))
            for g in heads:
                cols = slice(g * hw, (g + 1) * hw)
                kaug[g, c, :, 0:hw] = k_ref[r0:r0 + tk, cols]
                kaug[g, c, :, hw:2 * hw] = aux.astype(bf16)
                vt[g, c, 0:hw, :] = (
                    v_ref[r0:r0 + tk, cols].astype(f32).T.astype(bf16))
                vt[g, c, hw:DIFF_VROWS, :] = jnp.ones((DIFF_VROWS - hw, tk), bf16)

    def setup_queries(g, tile):
        q0 = pl.multiple_of(tile * tq, tq)
        q_t = (q_ref[pl.ds(q0, tq), g * hw:(g + 1) * hw].astype(f32)
               * (LOG2E * d ** -0.5)).T
        qrow = lax.broadcasted_iota(jnp.int32, (hw, tq), 0)
        qaug_t[g, 0:hw, 0:tq] = jnp.where(qrow < d, q_t, 0.0).astype(bf16)
        qaug_t[g, 0:hw, tq:2 * tq] = jnp.where(qrow >= d, q_t, 0.0).astype(bf16)
        a_coef = slopes_ref[hp * DIFF_HEADS_PER_STEP + g] * LOG2E
        c_coef = -a_coef * q0.astype(f32)
        r = lax.broadcasted_iota(jnp.int32, (16, 2 * tq), 0)
        t = jnp.where(r < 3, a_coef * 64.0,
                      jnp.where(r < 6, a_coef, jnp.where(r < 9, c_coef, 0.0)))
        t0 = t.astype(bf16).astype(f32)
        t1 = (t - t0).astype(bf16).astype(f32)
        t2 = t - t0 - t1
        first = (r == 0) | (r == 3) | (r == 6)
        second = (r == 1) | (r == 4) | (r == 7)
        qaug_t[g, hw:hw + 16, :] = jnp.where(
            first, t0, jnp.where(second, t1, t2)).astype(bf16)
        qaug_t[g, hw + 16:2 * hw, :] = jnp.zeros((hw - 16, 2 * tq), bf16)

    def scores(j, dst):
        for g in heads:
            dst[g] = jnp.dot(kaug[g, j], qaug_t[g], preferred_element_type=f32)

    def softmax_pv(j, src, masked):
        for g in heads:
            s = src[g]
            if masked:
                key = lax.broadcasted_iota(jnp.int32, (tk, 2 * tq), 0)
                qry = lax.broadcasted_iota(jnp.int32, (tk, 2 * tq), 1)
                qry = jnp.where(qry >= tq, qry - tq, qry)
                s = jnp.where(key <= qry, s, NEG_BIG)
            m_old = m_scr[g]
            m_new = jnp.maximum(m_old, jnp.max(s, axis=0, keepdims=True))
            alpha = jnp.exp2(m_old - m_new)
            p = jnp.exp2(s - m_new).astype(bf16)
            acc_scr[g] = acc_scr[g] * alpha + jnp.dot(
                vt[g, j], p, preferred_element_type=f32)
            m_scr[g] = m_new

    def normalise_and_store(qi):
        lam = (jnp.exp(jnp.sum(lamv_ref[0:1, :] * lamv_ref[1:2, :]))
               - jnp.exp(jnp.sum(lamv_ref[2:3, :] * lamv_ref[3:4, :]))
               + lambda_init)
        for g in heads:
            acc = acc_scr[g]
            o_all = acc[0:hw, :] * (1.0 / acc[hw:hw + 1, :])
            o_t = o_all[:, 0:tq] - lam * o_all[:, tq:2 * tq]
            o_t = o_t * lax.rsqrt(
                jnp.mean(o_t * o_t, axis=0, keepdims=True) + EPS)
            o = o_t.T * (subg_ref[...] * (1.0 - lambda_init))
            o_ref[pl.ds(pl.multiple_of(qi * tq, tq), tq),
                  g * hw:(g + 1) * hw] = o.astype(o_ref.dtype)

    def stage(j, src, dst):
        scores(j + 1, dst)
        softmax_pv(j, src, masked=False)

    def finish(qi, src):
        softmax_pv(qi, src, masked=True)
        for g in heads:
            setup_queries(g, jnp.minimum(qi + 1, nq - 1))
        scores(0, s_c)
        normalise_and_store(qi)

    def query_tile(qi, carry):
        acc_scr[...] = jnp.zeros(acc_scr.shape, f32)
        m_scr[...] = jnp.full(m_scr.shape, NEG_BIG, f32)

        @pl.when(qi >= 1)
        def _stage0():
            stage(0, s_c, s_a)

        def pair(i, c):
            stage(2 * i + 1, s_a, s_b)
            stage(2 * i + 2, s_b, s_a)
            return c

        lax.fori_loop(0, jnp.maximum(qi - 1, 0) // 2, pair, 0)

        even_tail = (qi % 2 == 0) & (qi >= 2)

        @pl.when(even_tail)
        def _last_unmasked():
            stage(qi - 1, s_a, s_b)

        @pl.when(even_tail)
        def _finish_even():
            finish(qi, s_b)

        @pl.when(qi % 2 == 1)
        def _finish_odd():
            finish(qi, s_a)

        @pl.when(qi == 0)
        def _finish_first():
            finish(qi, s_c)

        return carry

    stage_keys_values()
    for g in heads:
        setup_queries(g, 0)
    scores(0, s_c)
    lax.fori_loop(0, nq, query_tile, 0)


def diff_attention(qkv, slopes, lam_vecs, subln_g, lambda_init, batch, seq_len):
    m = qkv.shape[0]
    hw = 2 * DIFF_HEAD_DIM
    nq = seq_len // DIFF_TQ
    smem = pl.BlockSpec(memory_space=pltpu.SMEM)
    assert DIFF_TQ == DIFF_TK and seq_len % DIFF_TK == 0
    nk = seq_len // DIFF_TK
    hps = DIFF_HEADS_PER_STEP
    groups = DIFF_HEADS // hps
    bw = hps * hw
    score_buf = pltpu.VMEM((hps, DIFF_TK, 2 * DIFF_TQ), jnp.float32)
    return pl.pallas_call(
        functools.partial(_diff_kernel, lambda_init=lambda_init,
                          seq_len=seq_len),
        grid=(batch, groups),
        in_specs=[smem,
                  _const_spec(lam_vecs.shape),
                  _const_spec((1, hw)),
                  pl.BlockSpec((seq_len, bw), lambda b, h: (b, h)),
                  pl.BlockSpec((seq_len, bw), lambda b, h: (b, groups + h)),
                  pl.BlockSpec((seq_len, bw), lambda b, h: (b, 2 * groups + h))],
        out_specs=pl.BlockSpec((seq_len, bw), lambda b, h: (b, h)),
        out_shape=jax.ShapeDtypeStruct((m, DIFF_HEADS * hw), jnp.bfloat16),
        scratch_shapes=[pltpu.VMEM((hps, nk, DIFF_TK, 2 * hw), jnp.bfloat16),
                        pltpu.VMEM((hps, nk, DIFF_VROWS, DIFF_TK), jnp.bfloat16),
                        pltpu.VMEM((hps, 2 * hw, 2 * DIFF_TQ), jnp.bfloat16),
                        pltpu.VMEM((hps, DIFF_VROWS, 2 * DIFF_TQ), jnp.float32),
                        pltpu.VMEM((hps, 1, 2 * DIFF_TQ), jnp.float32),
                        score_buf, score_buf, score_buf],
        compiler_params=_params(("parallel", "parallel")),
        name="diff_attention",
    )(slopes, lam_vecs, subln_g.reshape(1, hw), qkv, qkv, qkv)


def _alibi_slopes(n_heads):
    return jnp.exp2(-8.0 * (jnp.arange(n_heads, dtype=jnp.float32) + 1.0) / n_heads)


def _diff_lambda_init(layer_idx):
    return 0.8 - 0.6 * math.exp(-0.3 * layer_idx)


def kernel(x, mix_pre_g, mix_post_g, ffn_pre_g, ffn_post_g, swa_w_qkv, swa_sinks, swa_w_o, diff_w_qkv, diff_lam_q1, diff_lam_k1, diff_lam_q2, diff_lam_k2, diff_subln_g, diff_w_o, ffn_w_up, ffn_conv_w, ffn_conv_b, ffn_w_down):
    batch, seq_len, d = x.shape
    depth = mix_pre_g.shape[0]
    bf = jnp.bfloat16
    x2d = x.reshape(batch * seq_len, d)
    for i in range(depth):
        j = i // N_MIXERS
        if i % N_MIXERS == 0:
            qkv = norm_matmul(x2d, mix_pre_g[i], swa_w_qkv[j].astype(bf),
                              n_chunk=640)
            a = swa_attention(qkv, _alibi_slopes(SWA_HEADS),
                              swa_sinks[j].astype(jnp.float32), batch, seq_len)
            w_o = swa_w_o[j]
        else:
            qkv = norm_matmul(x2d, mix_pre_g[i], diff_w_qkv[j].astype(bf))
            lam_vecs = jnp.stack([diff_lam_q1[j], diff_lam_k1[j],
                                  diff_lam_q2[j], diff_lam_k2[j]]).astype(jnp.float32)
            a = diff_attention(qkv, _alibi_slopes(DIFF_HEADS), lam_vecs,
                               diff_subln_g[j], _diff_lambda_init(i),
                               batch, seq_len)
            w_o = diff_w_o[j]
        x2d, act = mixer_out_ffn_up(a, w_o.astype(bf), mix_post_g[i], x2d,
                                    ffn_pre_g[i], ffn_w_up[i].astype(bf),
                                    ffn_conv_w[i], ffn_conv_b[i], seq_len)
        x2d = matmul_norm_residual(act, ffn_w_down[i].astype(bf), ffn_post_g[i], x2d)
    return x2d.reshape(batch, seq_len, d)
```

```python
import functools
import math

import jax
import jax.numpy as jnp
import numpy as np
from jax import lax
from jax.experimental import pallas as pl
from jax.experimental.pallas import tpu as pltpu

D_MODEL = 1024
EPS = 1e-6
BLOCK = 128
SWA_HEADS = 16
SWA_KV_HEADS = 2
SWA_HEAD_DIM = 64
SWA_GROUP = SWA_HEADS // SWA_KV_HEADS
DIFF_HEADS = 8
DIFF_HEAD_DIM = 64
D_FF = 2816
CONV_WIDTH = 3
N_MIXERS = 2

VMEM_LIMIT_BYTES = 56 * 1024 * 1024
LANES = 128
HALO = 16
GELU_C0 = math.sqrt(2.0 / math.pi)
GELU_C1 = 0.044715 * GELU_C0
NEG_BIG = -1e30

ROW_TILE = 1024
FFN_ROW_TILE = 1024
FF_CHUNK = 256
SWA_TILE = 1024
SWA_VROWS = 80
DIFF_TQ = 512
DIFF_TK = 512
DIFF_VROWS = 144
DIFF_HEADS_PER_STEP = 1
LOG2E = 1.4426950408889634


def _rms(x, g):
    return x * lax.rsqrt(jnp.mean(x * x, axis=-1, keepdims=True) + EPS) * g


def _params(sem):
    return pltpu.CompilerParams(dimension_semantics=sem,
                                vmem_limit_bytes=VMEM_LIMIT_BYTES)


def _layer_spec(shape, layer):
    nd = len(shape)
    return pl.BlockSpec((None,) + tuple(shape[1:]),
                        lambda *_: (layer,) + (0,) * (nd - 1),
                        pipeline_mode=pl.Buffered(1))


def _const_spec(shape):
    nd = len(shape)
    return pl.BlockSpec(shape, lambda *_: (0,) * nd,
                        pipeline_mode=pl.Buffered(1))


def _norm_matmul_kernel(x_ref, g_ref, w_ref, o_ref, *, n_chunk):
    h = _rms(x_ref[...], g_ref[...]).astype(jnp.bfloat16)
    n = o_ref.shape[1]
    for c in range(0, n, n_chunk):
        o_ref[:, c:c + n_chunk] = jnp.dot(
            h, w_ref[:, c:c + n_chunk],
            preferred_element_type=jnp.float32).astype(o_ref.dtype)


def norm_matmul(x2d, g, w_stack, layer, n_chunk=512):
    m, d = x2d.shape
    n = w_stack.shape[2]
    return pl.pallas_call(
        functools.partial(_norm_matmul_kernel, n_chunk=n_chunk),
        grid=(m // ROW_TILE,),
        in_specs=[pl.BlockSpec((ROW_TILE, d), lambda i: (i, 0)),
                  _const_spec((1, d)),
                  _layer_spec(w_stack.shape, layer)],
        out_specs=pl.BlockSpec((ROW_TILE, n), lambda i: (i, 0)),
        out_shape=jax.ShapeDtypeStruct((m, n), jnp.bfloat16),
        compiler_params=_params(("parallel",)),
        name="norm_matmul",
    )(x2d, g.reshape(1, d), w_stack)


def _matmul_norm_res_kernel(a_ref, w_ref, g_ref, x_ref, o_ref):
    y = jnp.dot(a_ref[...], w_ref[...], preferred_element_type=jnp.float32)
    o_ref[...] = x_ref[...] + _rms(y, g_ref[...])


def matmul_norm_residual(a_bf16, w_stack, layer, g, x2d):
    m, k = a_bf16.shape
    d = w_stack.shape[2]
    return pl.pallas_call(
        _matmul_norm_res_kernel,
        grid=(m // ROW_TILE,),
        in_specs=[pl.BlockSpec((ROW_TILE, k), lambda i: (i, 0)),
                  _layer_spec(w_stack.shape, layer),
                  _const_spec((1, d)),
                  pl.BlockSpec((ROW_TILE, d), lambda i: (i, 0))],
        out_specs=pl.BlockSpec((ROW_TILE, d), lambda i: (i, 0)),
        out_shape=jax.ShapeDtypeStruct((m, d), jnp.float32),
        compiler_params=_params(("parallel",)),
        name="matmul_norm_residual",
    )(a_bf16, w_stack, g.reshape(1, d), x2d)


def _ffn_up_kernel(ahalo_ref, a_ref, wo_ref, gpost_ref, halo_ref, x_ref, g_ref,
                   wup_ref, cw_ref, cb_ref, x1_ref, o_ref,
                   h_scr, ug_scr, uv_scr, *, seq_len):
    i = pl.program_id(0)
    tm = x_ref.shape[0]
    a_ext = jnp.concatenate([ahalo_ref[...], a_ref[...]], axis=0)
    y = jnp.dot(a_ext, wo_ref[...], preferred_element_type=jnp.float32)
    x_ext = jnp.concatenate([halo_ref[...], x_ref[...]], axis=0)
    x1 = x_ext + _rms(y, gpost_ref[...])
    x1_ref[...] = x1[HALO:, :]
    g = g_ref[...]
    first = (i * tm) % seq_len == 0
    h_scr[0:HALO, :] = _rms(jnp.where(first, 0.0, x1[0:HALO, :]),
                            g).astype(jnp.bfloat16)
    h_scr[HALO:, :] = _rms(x1[HALO:, :], g).astype(jnp.bfloat16)

    def conv(u_scr, slab, col):
        c = cb_ref[:, col:col + LANES]
        for tap in range(CONV_WIDTH):
            off = HALO - (CONV_WIDTH - 1) + tap
            rows = pl.ds(off, tm) if off % 8 == 0 else pl.ds(off, tm, stride=1)
            c = c + cw_ref[tap:tap + 1, col:col + LANES] * u_scr[slab, rows, :]
        return c

    slabs = FF_CHUNK // LANES
    for j in range(D_FF // FF_CHUNK):
        cg = j * FF_CHUNK
        cv = D_FF + cg
        ug = jnp.dot(h_scr[...], wup_ref[:, cg:cg + FF_CHUNK],
                     preferred_element_type=jnp.float32)
        uv = jnp.dot(h_scr[...], wup_ref[:, cv:cv + FF_CHUNK],
                     preferred_element_type=jnp.float32)
        base = (j % 2) * slabs
        for s in range(slabs):
            ug_scr[base + s] = ug[:, s * LANES:(s + 1) * LANES]
            uv_scr[base + s] = uv[:, s * LANES:(s + 1) * LANES]
        for s in range(slabs):
            gate = conv(ug_scr, base + s, cg + s * LANES)
            val = conv(uv_scr, base + s, cv + s * LANES)
            t = jnp.tanh(gate * (GELU_C0 + GELU_C1 * (gate * gate)))
            o_ref[:, cg + s * LANES:cg + (s + 1) * LANES] = (
                (gate * val) * (0.5 * t + 0.5)).astype(o_ref.dtype)


def mixer_out_ffn_up(a_bf16, wo_stack, wo_layer, g_post, x2d, g, wup_stack,
                     conv_w_stack, layer, conv_b, seq_len):
    m, d = x2d.shape
    ka = a_bf16.shape[1]
    n2 = wup_stack.shape[2]
    tm = FFN_ROW_TILE
    halo_blocks = tm // HALO
    u_scratch = pltpu.VMEM((2 * FF_CHUNK // LANES, tm + HALO, LANES), jnp.float32)

    def halo_map(i):
        return (jnp.maximum(i * halo_blocks - 1, 0), 0)

    def tile_map(i):
        return (i, 0)

    return pl.pallas_call(
        functools.partial(_ffn_up_kernel, seq_len=seq_len),
        grid=(m // tm,),
        in_specs=[pl.BlockSpec((HALO, ka), halo_map),
                  pl.BlockSpec((tm, ka), tile_map),
                  _layer_spec(wo_stack.shape, wo_layer),
                  _const_spec((1, d)),
                  pl.BlockSpec((HALO, d), halo_map),
                  pl.BlockSpec((tm, d), tile_map),
                  _const_spec((1, d)),
                  _layer_spec(wup_stack.shape, layer),
                  _layer_spec(conv_w_stack.shape, layer),
                  _const_spec((1, n2))],
        out_specs=[pl.BlockSpec((tm, d), tile_map),
                   pl.BlockSpec((tm, D_FF), tile_map)],
        out_shape=[jax.ShapeDtypeStruct((m, d), jnp.float32),
                   jax.ShapeDtypeStruct((m, D_FF), jnp.bfloat16)],
        scratch_shapes=[pltpu.VMEM((tm + HALO, d), jnp.bfloat16),
                        u_scratch, u_scratch],
        compiler_params=_params(("parallel",)),
        name="mixer_out_ffn_up",
    )(a_bf16, a_bf16, wo_stack, g_post.reshape(1, d), x2d, x2d, g.reshape(1, d),
      wup_stack, conv_w_stack, conv_b.reshape(1, n2))


def _swa_kernel(slopes_ref, sinks_ref, q_ref, k_ref, kp_ref, v_ref, vp_ref,
                o_ref, qt_all, k_aug, vt_aug, p_scr):
    t = pl.program_id(1)
    f32, bf16 = jnp.float32, jnp.bfloat16
    d, grp = SWA_HEAD_DIM, SWA_GROUP
    ncol = SWA_HEADS * BLOCK
    kvw = SWA_KV_HEADS * d

    @pl.when((pl.program_id(0) == 0) & (t == 0))
    def _init_constants():
        colh = lax.broadcasted_iota(jnp.int32, (16, ncol), 1) >> 7
        r = lax.broadcasted_iota(jnp.int32, (16, ncol), 0)
        a = jnp.zeros((16, ncol), f32)
        for h in range(SWA_HEADS):
            a = jnp.where(colh == h, slopes_ref[h] * LOG2E, a)
        a0 = a.astype(bf16).astype(f32)
        a1 = (a - a0).astype(bf16).astype(f32)
        a2 = a - a0 - a1
        coef = jnp.where(r == 0, a0, jnp.where(r == 1, a1,
                                               jnp.where(r == 2, a2, 0.0)))
        kj = lax.broadcasted_iota(jnp.int32, (2 * BLOCK, kvw), 0).astype(f32)
        lane = lax.broadcasted_iota(jnp.int32, (2 * BLOCK, kvw), 1)
        for par in range(SWA_TILE // BLOCK):
            qt_all[par] = jnp.zeros((2 * kvw, ncol), bf16)
            qt_all[par, kvw:kvw + 16, :] = coef.astype(bf16)
            k_aug[par, :, kvw:2 * kvw] = jnp.where(lane < 3, kj, 0.0).astype(bf16)
            for kv in range(SWA_KV_HEADS):
                vt_aug[par, kv, d:SWA_VROWS, :] = jnp.ones(
                    (SWA_VROWS - d, 2 * BLOCK), bf16)

    key = lax.broadcasted_iota(jnp.int32, (2 * BLOCK, BLOCK), 0)
    qry = lax.broadcasted_iota(jnp.int32, (2 * BLOCK, BLOCK), 1)
    dist = qry + BLOCK - key
    band = (dist >= 0) & (dist < BLOCK)
    band_first = band & (key >= jnp.where(t > 0, 0, BLOCK))
    qpos = (lax.broadcasted_iota(jnp.int32, (1, BLOCK), 1) + BLOCK).astype(f32)

    for blk in range(SWA_TILE // BLOCK):
        par = blk
        r0 = blk * BLOCK
        valid = band_first if blk == 0 else band
        k_prev = kp_ref[...] if blk == 0 else k_ref[r0 - BLOCK:r0, :]
        v_prev = vp_ref[...] if blk == 0 else v_ref[r0 - BLOCK:r0, :]
        k_aug[par, 0:BLOCK, 0:kvw] = k_prev
        k_aug[par, BLOCK:2 * BLOCK, 0:kvw] = k_ref[r0:r0 + BLOCK, :]
        v_t = jnp.concatenate([v_prev, v_ref[r0:r0 + BLOCK, :]],
                              axis=0).astype(f32).T
        for kv in range(SWA_KV_HEADS):
            vt_aug[par, kv, 0:d, :] = v_t[kv * d:(kv + 1) * d, :].astype(bf16)
        q_t = (q_ref[r0:r0 + BLOCK, :].astype(f32) * (LOG2E * d ** -0.5)).T
        for h in range(SWA_HEADS):
            kv = h // grp
            qt_all[par, kv * d:(kv + 1) * d, h * BLOCK:(h + 1) * BLOCK] = (
                q_t[h * d:(h + 1) * d, :].astype(bf16))

        s = jnp.dot(k_aug[par], qt_all[par], preferred_element_type=f32)
        m_list, sink_list = [], []
        for h in range(SWA_HEADS):
            s_h = jnp.where(valid, s[:, h * BLOCK:(h + 1) * BLOCK], NEG_BIG)
            sink_h = (sinks_ref[h] * LOG2E
                      + (slopes_ref[h] * LOG2E) * qpos)
            m_h = jnp.maximum(jnp.max(s_h, axis=0, keepdims=True), sink_h)
            p_scr[par, :, h * BLOCK:(h + 1) * BLOCK] = (
                jnp.exp2(s_h - m_h).astype(bf16))
            m_list.append(m_h)
            sink_list.append(sink_h)

        for kv in range(SWA_KV_HEADS):
            pv = jnp.dot(vt_aug[par, kv],
                         p_scr[par, :, kv * grp * BLOCK:(kv + 1) * grp * BLOCK],
                         preferred_element_type=f32)
            for pair in range(grp // 2):
                halves = []
                for g in (2 * pair, 2 * pair + 1):
                    h = kv * grp + g
                    cols = slice(g * BLOCK, (g + 1) * BLOCK)
                    denom = pv[d:d + 1, cols] + jnp.exp2(
                        sink_list[h] - m_list[h])
                    halves.append(pv[0:d, cols] / denom)
                o_pair = jnp.concatenate(halves, axis=0).T
                c0 = (kv * grp + 2 * pair) * d
                o_ref[r0:r0 + BLOCK, c0:c0 + 2 * d] = o_pair.astype(o_ref.dtype)


def swa_attention(qkv, slopes, sinks, batch, seq_len):
    m = qkv.shape[0]
    nt = seq_len // SWA_TILE
    per_tile = SWA_TILE // BLOCK
    hq = SWA_HEADS * SWA_HEAD_DIM
    kv_w = SWA_KV_HEADS * SWA_HEAD_DIM
    k_col = hq // kv_w
    v_col = k_col + 1
    ncol = SWA_HEADS * BLOCK

    def cur(col):
        return lambda b, t: (b * nt + t, col)

    def prev(col):
        return lambda b, t: (jnp.maximum((b * nt + t) * per_tile - 1, 0), col)

    smem = pl.BlockSpec(memory_space=pltpu.SMEM)
    return pl.pallas_call(
        _swa_kernel,
        grid=(batch, nt),
        in_specs=[smem, smem,
                  pl.BlockSpec((SWA_TILE, hq), cur(0)),
                  pl.BlockSpec((SWA_TILE, kv_w), cur(k_col)),
                  pl.BlockSpec((BLOCK, kv_w), prev(k_col)),
                  pl.BlockSpec((SWA_TILE, kv_w), cur(v_col)),
                  pl.BlockSpec((BLOCK, kv_w), prev(v_col))],
        out_specs=pl.BlockSpec((SWA_TILE, hq), cur(0)),
        out_shape=jax.ShapeDtypeStruct((m, hq), jnp.bfloat16),
        scratch_shapes=[pltpu.VMEM((per_tile, 2 * kv_w, ncol), jnp.bfloat16),
                        pltpu.VMEM((per_tile, 2 * BLOCK, 2 * kv_w), jnp.bfloat16),
                        pltpu.VMEM((per_tile, SWA_KV_HEADS, SWA_VROWS, 2 * BLOCK),
                                   jnp.bfloat16),
                        pltpu.VMEM((per_tile, 2 * BLOCK, ncol), jnp.bfloat16)],
        compiler_params=_params(("arbitrary", "arbitrary")),
        name="swa_attention",
    )(slopes, sinks, qkv, qkv, qkv, qkv, qkv)


def _diff_kernel(slopes_ref, lamv_ref, subg_ref, q_ref, k_ref, v_ref, o_ref,
                 kaug, vt, qaug_t, acc_scr, m_scr, s_a, s_b, s_c, *,
                 lambda_init, seq_len):
    hp = pl.program_id(1)
    tq, tk, d = DIFF_TQ, DIFF_TK, DIFF_HEAD_DIM
    hw = 2 * d
    nq = seq_len // tq
    f32, bf16 = jnp.float32, jnp.bfloat16
    heads = range(DIFF_HEADS_PER_STEP)

    def stage_keys_values():
        lane = lax.broadcasted_iota(jnp.int32, (tk, hw), 1)
        row = lax.broadcasted_iota(jnp.int32, (tk, hw), 0)
        for c in range(seq_len // tk):
            r0 = c * tk
            pos = row + r0
            hi = (pos >> 6).astype(f32)
            lo = (pos & 63).astype(f32)
            aux = jnp.where(lane < 3, hi,
                            jnp.where(lane < 6, lo,
                                      jnp.where(lane < 9, 1.0, 0.0)))
            for g in heads:
                cols = slice(g * hw, (g + 1) * hw)
                kaug[g, c, :, 0:hw] = k_ref[r0:r0 + tk, cols]
                kaug[g, c, :, hw:2 * hw] = aux.astype(bf16)
                vt[g, c, 0:hw, :] = (
                    v_ref[r0:r0 + tk, cols].astype(f32).T.astype(bf16))
                vt[g, c, hw:DIFF_VROWS, :] = jnp.ones((DIFF_VROWS - hw, tk), bf16)

    def setup_queries(g, tile):
        q0 = pl.multiple_of(tile * tq, tq)
        q_t = (q_ref[pl.ds(q0, tq), g * hw:(g + 1) * hw].astype(f32)
               * (LOG2E * d ** -0.5)).T
        qrow = lax.broadcasted_iota(jnp.int32, (hw, tq), 0)
        qaug_t[g, 0:hw, 0:tq] = jnp.where(qrow < d, q_t, 0.0).astype(bf16)
        qaug_t[g, 0:hw, tq:2 * tq] = jnp.where(qrow >= d, q_t, 0.0).astype(bf16)
        a_coef = slopes_ref[hp * DIFF_HEADS_PER_STEP + g] * LOG2E
        c_coef = -a_coef * q0.astype(f32)
        r = lax.broadcasted_iota(jnp.int32, (16, 2 * tq), 0)
        t = jnp.where(r < 3, a_coef * 64.0,
                      jnp.where(r < 6, a_coef, jnp.where(r < 9, c_coef, 0.0)))
        t0 = t.astype(bf16).astype(f32)
        t1 = (t - t0).astype(bf16).astype(f32)
        t2 = t - t0 - t1
        first = (r == 0) | (r == 3) | (r == 6)
        second = (r == 1) | (r == 4) | (r == 7)
        qaug_t[g, hw:hw + 16, :] = jnp.where(
            first, t0, jnp.where(second, t1, t2)).astype(bf16)
        qaug_t[g, hw + 16:2 * hw, :] = jnp.zeros((hw - 16, 2 * tq), bf16)

    def scores(j, dst):
        for g in heads:
            s = jnp.dot(kaug[g, j], qaug_t[g], preferred_element_type=f32)
            dst[g, 0:tk, :] = s
            dst[g, tk:tk + 1, :] = jnp.max(s, axis=0, keepdims=True)

    def softmax_pv(j, src, masked):
        for g in heads:
            s = src[g, 0:tk, :]
            if masked:
                key = lax.broadcasted_iota(jnp.int32, (tk, 2 * tq), 0)
                qry = lax.broadcasted_iota(jnp.int32, (tk, 2 * tq), 1)
                qry = jnp.where(qry >= tq, qry - tq, qry)
                s = jnp.where(key <= qry, s, NEG_BIG)
                col_max = jnp.max(s, axis=0, keepdims=True)
            else:
                col_max = src[g, tk:tk + 1, :]
            m_old = m_scr[g]
            m_new = jnp.maximum(m_old, col_max)
            alpha = jnp.exp2(m_old - m_new)
            p = jnp.exp2(s - m_new).astype(bf16)
            acc_scr[g] = acc_scr[g] * alpha + jnp.dot(
                vt[g, j], p, preferred_element_type=f32)
            m_scr[g] = m_new

    def normalise_and_store(qi):
        lam = (jnp.exp(jnp.sum(lamv_ref[0:1, :] * lamv_ref[1:2, :]))
               - jnp.exp(jnp.sum(lamv_ref[2:3, :] * lamv_ref[3:4, :]))
               + lambda_init)
        for g in heads:
            acc = acc_scr[g]
            o_all = acc[0:hw, :] * (1.0 / acc[hw:hw + 1, :])
            o_t = o_all[:, 0:tq] - lam * o_all[:, tq:2 * tq]
            o_t = o_t * lax.rsqrt(
                jnp.mean(o_t * o_t, axis=0, keepdims=True) + EPS)
            o = o_t.T * (subg_ref[...] * (1.0 - lambda_init))
            o_ref[pl.ds(pl.multiple_of(qi * tq, tq), tq),
                  g * hw:(g + 1) * hw] = o.astype(o_ref.dtype)

    def stage(j, src, dst):
        scores(j + 1, dst)
        softmax_pv(j, src, masked=False)

    def finish(qi, src):
        softmax_pv(qi, src, masked=True)
        for g in heads:
            setup_queries(g, jnp.minimum(qi + 1, nq - 1))
        scores(0, s_c)
        normalise_and_store(qi)

    def query_tile(qi, carry):
        acc_scr[...] = jnp.zeros(acc_scr.shape, f32)
        m_scr[...] = jnp.full(m_scr.shape, NEG_BIG, f32)

        @pl.when(qi >= 1)
        def _stage0():
            stage(0, s_c, s_a)

        def pair(i, c):
            stage(2 * i + 1, s_a, s_b)
            stage(2 * i + 2, s_b, s_a)
            return c

        lax.fori_loop(0, jnp.maximum(qi - 1, 0) // 2, pair, 0)

        even_tail = (qi % 2 == 0) & (qi >= 2)

        @pl.when(even_tail)
        def _last_unmasked():
            stage(qi - 1, s_a, s_b)

        @pl.when(even_tail)
        def _finish_even():
            finish(qi, s_b)

        @pl.when(qi % 2 == 1)
        def _finish_odd():
            finish(qi, s_a)

        @pl.when(qi == 0)
        def _finish_first():
            finish(qi, s_c)

        return carry

    stage_keys_values()
    for g in heads:
        setup_queries(g, 0)
    scores(0, s_c)
    lax.fori_loop(0, nq, query_tile, 0)


def diff_attention(qkv, slopes, lam_vecs, subln_g, lambda_init, batch, seq_len):
    m = qkv.shape[0]
    hw = 2 * DIFF_HEAD_DIM
    nq = seq_len // DIFF_TQ
    smem = pl.BlockSpec(memory_space=pltpu.SMEM)
    assert DIFF_TQ == DIFF_TK and seq_len % DIFF_TK == 0
    nk = seq_len // DIFF_TK
    hps = DIFF_HEADS_PER_STEP
    groups = DIFF_HEADS // hps
    bw = hps * hw
    score_buf = pltpu.VMEM((hps, DIFF_TK + 8, 2 * DIFF_TQ), jnp.float32)
    return pl.pallas_call(
        functools.partial(_diff_kernel, lambda_init=lambda_init,
                          seq_len=seq_len),
        grid=(batch, groups),
        in_specs=[smem,
                  _const_spec(lam_vecs.shape),
                  _const_spec((1, hw)),
                  pl.BlockSpec((seq_len, bw), lambda b, h: (b, h)),
                  pl.BlockSpec((seq_len, bw), lambda b, h: (b, groups + h)),
                  pl.BlockSpec((seq_len, bw), lambda b, h: (b, 2 * groups + h))],
        out_specs=pl.BlockSpec((seq_len, bw), lambda b, h: (b, h)),
        out_shape=jax.ShapeDtypeStruct((m, DIFF_HEADS * hw), jnp.bfloat16),
        scratch_shapes=[pltpu.VMEM((hps, nk, DIFF_TK, 2 * hw), jnp.bfloat16),
                        pltpu.VMEM((hps, nk, DIFF_VROWS, DIFF_TK), jnp.bfloat16),
                        pltpu.VMEM((hps, 2 * hw, 2 * DIFF_TQ), jnp.bfloat16),
                        pltpu.VMEM((hps, DIFF_VROWS, 2 * DIFF_TQ), jnp.float32),
                        pltpu.VMEM((hps, 1, 2 * DIFF_TQ), jnp.float32),
                        score_buf, score_buf, score_buf],
        compiler_params=_params(("parallel", "parallel")),
        name="diff_attention",
    )(slopes, lam_vecs, subln_g.reshape(1, hw), qkv, qkv, qkv)


def _alibi_slopes(n_heads):
    return np.exp2(-8.0 * (np.arange(n_heads, dtype=np.float32) + 1.0)
                   / n_heads).astype(np.float32)


def _diff_lambda_init(layer_idx):
    return 0.8 - 0.6 * math.exp(-0.3 * layer_idx)


def kernel(x, mix_pre_g, mix_post_g, ffn_pre_g, ffn_post_g, swa_w_qkv, swa_sinks, swa_w_o, diff_w_qkv, diff_lam_q1, diff_lam_k1, diff_lam_q2, diff_lam_k2, diff_subln_g, diff_w_o, ffn_w_up, ffn_conv_w, ffn_conv_b, ffn_w_down):
    batch, seq_len, d = x.shape
    depth = mix_pre_g.shape[0]
    bf = jnp.bfloat16
    swa_w_qkv, swa_w_o = swa_w_qkv.astype(bf), swa_w_o.astype(bf)
    diff_w_qkv, diff_w_o = diff_w_qkv.astype(bf), diff_w_o.astype(bf)
    ffn_w_up, ffn_w_down = ffn_w_up.astype(bf), ffn_w_down.astype(bf)
    x2d = x.reshape(batch * seq_len, d)
    for i in range(depth):
        j = i // N_MIXERS
        if i % N_MIXERS == 0:
            qkv = norm_matmul(x2d, mix_pre_g[i], swa_w_qkv, j, n_chunk=640)
            a = swa_attention(qkv, _alibi_slopes(SWA_HEADS),
                              swa_sinks[j].astype(jnp.float32), batch, seq_len)
            w_o = swa_w_o
        else:
            qkv = norm_matmul(x2d, mix_pre_g[i], diff_w_qkv, j)
            lam_vecs = jnp.stack([diff_lam_q1[j], diff_lam_k1[j],
                                  diff_lam_q2[j], diff_lam_k2[j]]).astype(jnp.float32)
            a = diff_attention(qkv, _alibi_slopes(DIFF_HEADS), lam_vecs,
                               diff_subln_g[j], _diff_lambda_init(i),
                               batch, seq_len)
            w_o = diff_w_o
        x2d, act = mixer_out_ffn_up(a, w_o, j, mix_post_g[i], x2d, ffn_pre_g[i],
                                    ffn_w_up, ffn_conv_w, i, ffn_conv_b[i],
                                    seq_len)
        x2d = matmul_norm_residual(act, ffn_w_down, i, ffn_post_g[i], x2d)
    return x2d.reshape(batch, seq_len, d)
```

```python
import functools
import math

import jax
import jax.numpy as jnp
import numpy as np
from jax import lax
from jax.experimental import pallas as pl
from jax.experimental.pallas import tpu as pltpu

D_MODEL = 1024
EPS = 1e-6
BLOCK = 128
SWA_HEADS = 16
SWA_KV_HEADS = 2
SWA_HEAD_DIM = 64
SWA_GROUP = SWA_HEADS // SWA_KV_HEADS
DIFF_HEADS = 8
DIFF_HEAD_DIM = 64
D_FF = 2816
CONV_WIDTH = 3
N_MIXERS = 2

VMEM_LIMIT_BYTES = 56 * 1024 * 1024
LANES = 128
HALO = 16
GELU_C0 = math.sqrt(2.0 / math.pi)
GELU_C1 = 0.044715 * GELU_C0
NEG_BIG = -1e30

ROW_TILE = 1024
FFN_ROW_TILE = 1024
FF_CHUNK = 256
SWA_TILE = 1024
SWA_VROWS = 80
DIFF_TQ = 512
DIFF_TK = 512
DIFF_VROWS = 144
DIFF_HEADS_PER_STEP = 1
LOG2E = 1.4426950408889634


def _rms(x, g):
    return x * lax.rsqrt(jnp.mean(x * x, axis=-1, keepdims=True) + EPS) * g


def _params(sem):
    return pltpu.CompilerParams(dimension_semantics=sem,
                                vmem_limit_bytes=VMEM_LIMIT_BYTES)


def _layer_spec(shape, layer):
    nd = len(shape)
    return pl.BlockSpec((None,) + tuple(shape[1:]),
                        lambda *_: (layer,) + (0,) * (nd - 1),
                        pipeline_mode=pl.Buffered(1))


def _const_spec(shape):
    nd = len(shape)
    return pl.BlockSpec(shape, lambda *_: (0,) * nd,
                        pipeline_mode=pl.Buffered(1))


def _norm_matmul_kernel(x_ref, g_ref, w_ref, o_ref, *, n_chunk):
    h = _rms(x_ref[...], g_ref[...]).astype(jnp.bfloat16)
    n = o_ref.shape[1]
    for c in range(0, n, n_chunk):
        o_ref[:, c:c + n_chunk] = jnp.dot(
            h, w_ref[:, c:c + n_chunk],
            preferred_element_type=jnp.float32).astype(o_ref.dtype)


def norm_matmul(x2d, g, w_stack, layer, n_chunk=512):
    m, d = x2d.shape
    n = w_stack.shape[2]
    return pl.pallas_call(
        functools.partial(_norm_matmul_kernel, n_chunk=n_chunk),
        grid=(m // ROW_TILE,),
        in_specs=[pl.BlockSpec((ROW_TILE, d), lambda i: (i, 0)),
                  _const_spec((1, d)),
                  _layer_spec(w_stack.shape, layer)],
        out_specs=pl.BlockSpec((ROW_TILE, n), lambda i: (i, 0)),
        out_shape=jax.ShapeDtypeStruct((m, n), jnp.bfloat16),
        compiler_params=_params(("parallel",)),
        name="norm_matmul",
    )(x2d, g.reshape(1, d), w_stack)


def _matmul_norm_res_kernel(a_ref, w_ref, g_ref, x_ref, o_ref):
    y = jnp.dot(a_ref[...], w_ref[...], preferred_element_type=jnp.float32)
    o_ref[...] = x_ref[...] + _rms(y, g_ref[...])


def matmul_norm_residual(a_bf16, w_stack, layer, g, x2d):
    m, k = a_bf16.shape
    d = w_stack.shape[2]
    return pl.pallas_call(
        _matmul_norm_res_kernel,
        grid=(m // ROW_TILE,),
        in_specs=[pl.BlockSpec((ROW_TILE, k), lambda i: (i, 0)),
                  _layer_spec(w_stack.shape, layer),
                  _const_spec((1, d)),
                  pl.BlockSpec((ROW_TILE, d), lambda i: (i, 0))],
        out_specs=pl.BlockSpec((ROW_TILE, d), lambda i: (i, 0)),
        out_shape=jax.ShapeDtypeStruct((m, d), jnp.float32),
        compiler_params=_params(("parallel",)),
        name="matmul_norm_residual",
    )(a_bf16, w_stack, g.reshape(1, d), x2d)


def _ffn_up_kernel(ahalo_ref, a_ref, wo_ref, gpost_ref, halo_ref, x_ref, g_ref,
                   wup_ref, cw_ref, cb_ref, x1_ref, o_ref,
                   h_scr, ug_scr, uv_scr, *, seq_len):
    i = pl.program_id(0)
    tm = x_ref.shape[0]
    a_ext = jnp.concatenate([ahalo_ref[...], a_ref[...]], axis=0)
    y = jnp.dot(a_ext, wo_ref[...], preferred_element_type=jnp.float32)
    x_ext = jnp.concatenate([halo_ref[...], x_ref[...]], axis=0)
    x1 = x_ext + _rms(y, gpost_ref[...])
    x1_ref[...] = x1[HALO:, :]
    g = g_ref[...]
    first = (i * tm) % seq_len == 0
    h_scr[0:HALO, :] = _rms(jnp.where(first, 0.0, x1[0:HALO, :]),
                            g).astype(jnp.bfloat16)
    h_scr[HALO:, :] = _rms(x1[HALO:, :], g).astype(jnp.bfloat16)

    def conv(u_scr, slab, col):
        c = cb_ref[:, col:col + LANES]
        for tap in range(CONV_WIDTH):
            off = HALO - (CONV_WIDTH - 1) + tap
            rows = pl.ds(off, tm) if off % 8 == 0 else pl.ds(off, tm, stride=1)
            c = c + cw_ref[tap:tap + 1, col:col + LANES] * u_scr[slab, rows, :]
        return c

    slabs = FF_CHUNK // LANES
    for j in range(D_FF // FF_CHUNK):
        cg = j * FF_CHUNK
        cv = D_FF + cg
        ug = jnp.dot(h_scr[...], wup_ref[:, cg:cg + FF_CHUNK],
                     preferred_element_type=jnp.float32)
        uv = jnp.dot(h_scr[...], wup_ref[:, cv:cv + FF_CHUNK],
                     preferred_element_type=jnp.float32)
        base = (j % 2) * slabs
        for s in range(slabs):
            ug_scr[base + s] = ug[:, s * LANES:(s + 1) * LANES]
            uv_scr[base + s] = uv[:, s * LANES:(s + 1) * LANES]
        for s in range(slabs):
            gate = conv(ug_scr, base + s, cg + s * LANES)
            val = conv(uv_scr, base + s, cv + s * LANES)
            t = jnp.tanh(gate * (GELU_C0 + GELU_C1 * (gate * gate)))
            o_ref[:, cg + s * LANES:cg + (s + 1) * LANES] = (
                (gate * val) * (0.5 * t + 0.5)).astype(o_ref.dtype)


def mixer_out_ffn_up(a_bf16, wo_stack, wo_layer, g_post, x2d, g, wup_stack,
                     conv_w_stack, layer, conv_b, seq_len):
    m, d = x2d.shape
    ka = a_bf16.shape[1]
    n2 = wup_stack.shape[2]
    tm = FFN_ROW_TILE
    halo_blocks = tm // HALO
    u_scratch = pltpu.VMEM((2 * FF_CHUNK // LANES, tm + HALO, LANES), jnp.float32)

    def halo_map(i):
        return (jnp.maximum(i * halo_blocks - 1, 0), 0)

    def tile_map(i):
        return (i, 0)

    return pl.pallas_call(
        functools.partial(_ffn_up_kernel, seq_len=seq_len),
        grid=(m // tm,),
        in_specs=[pl.BlockSpec((HALO, ka), halo_map),
                  pl.BlockSpec((tm, ka), tile_map),
                  _layer_spec(wo_stack.shape, wo_layer),
                  _const_spec((1, d)),
                  pl.BlockSpec((HALO, d), halo_map),
                  pl.BlockSpec((tm, d), tile_map),
                  _const_spec((1, d)),
                  _layer_spec(wup_stack.shape, layer),
                  _layer_spec(conv_w_stack.shape, layer),
                  _const_spec((1, n2))],
        out_specs=[pl.BlockSpec((tm, d), tile_map),
                   pl.BlockSpec((tm, D_FF), tile_map)],
        out_shape=[jax.ShapeDtypeStruct((m, d), jnp.float32),
                   jax.ShapeDtypeStruct((m, D_FF), jnp.bfloat16)],
        scratch_shapes=[pltpu.VMEM((tm + HALO, d), jnp.bfloat16),
                        u_scratch, u_scratch],
        compiler_params=_params(("parallel",)),
        name="mixer_out_ffn_up",
    )(a_bf16, a_bf16, wo_stack, g_post.reshape(1, d), x2d, x2d, g.reshape(1, d),
      wup_stack, conv_w_stack, conv_b.reshape(1, n2))


def _swa_kernel(slopes_ref, sinks_ref, q_ref, k_ref, kp_ref, v_ref, vp_ref,
                o_ref, qt_all, k_aug, vt_aug, p_scr):
    t = pl.program_id(1)
    f32, bf16 = jnp.float32, jnp.bfloat16
    d, grp = SWA_HEAD_DIM, SWA_GROUP
    ncol = SWA_HEADS * BLOCK
    kvw = SWA_KV_HEADS * d

    @pl.when((pl.program_id(0) == 0) & (t == 0))
    def _init_constants():
        colh = lax.broadcasted_iota(jnp.int32, (16, ncol), 1) >> 7
        r = lax.broadcasted_iota(jnp.int32, (16, ncol), 0)
        a = jnp.zeros((16, ncol), f32)
        for h in range(SWA_HEADS):
            a = jnp.where(colh == h, slopes_ref[h] * LOG2E, a)
        a0 = a.astype(bf16).astype(f32)
        a1 = (a - a0).astype(bf16).astype(f32)
        a2 = a - a0 - a1
        coef = jnp.where(r == 0, a0, jnp.where(r == 1, a1,
                                               jnp.where(r == 2, a2, 0.0)))
        kj = lax.broadcasted_iota(jnp.int32, (2 * BLOCK, kvw), 0).astype(f32)
        lane = lax.broadcasted_iota(jnp.int32, (2 * BLOCK, kvw), 1)
        for par in range(SWA_TILE // BLOCK):
            qt_all[par] = jnp.zeros((2 * kvw, ncol), bf16)
            qt_all[par, kvw:kvw + 16, :] = coef.astype(bf16)
            k_aug[par, :, kvw:2 * kvw] = jnp.where(lane < 3, kj, 0.0).astype(bf16)
            for kv in range(SWA_KV_HEADS):
                vt_aug[par, kv, d:SWA_VROWS, :] = jnp.ones(
                    (SWA_VROWS - d, 2 * BLOCK), bf16)

    key = lax.broadcasted_iota(jnp.int32, (2 * BLOCK, BLOCK), 0)
    qry = lax.broadcasted_iota(jnp.int32, (2 * BLOCK, BLOCK), 1)
    dist = qry + BLOCK - key
    band = (dist >= 0) & (dist < BLOCK)
    band_first = band & (key >= jnp.where(t > 0, 0, BLOCK))
    qpos = (lax.broadcasted_iota(jnp.int32, (1, BLOCK), 1) + BLOCK).astype(f32)

    for blk in range(SWA_TILE // BLOCK):
        par = blk
        r0 = blk * BLOCK
        valid = band_first if blk == 0 else band
        k_prev = kp_ref[...] if blk == 0 else k_ref[r0 - BLOCK:r0, :]
        v_prev = vp_ref[...] if blk == 0 else v_ref[r0 - BLOCK:r0, :]
        k_aug[par, 0:BLOCK, 0:kvw] = k_prev
        k_aug[par, BLOCK:2 * BLOCK, 0:kvw] = k_ref[r0:r0 + BLOCK, :]
        v_t = jnp.concatenate([v_prev, v_ref[r0:r0 + BLOCK, :]],
                              axis=0).astype(f32).T
        for kv in range(SWA_KV_HEADS):
            vt_aug[par, kv, 0:d, :] = v_t[kv * d:(kv + 1) * d, :].astype(bf16)
        q_t = (q_ref[r0:r0 + BLOCK, :].astype(f32) * (LOG2E * d ** -0.5)).T
        for h in range(SWA_HEADS):
            kv = h // grp
            qt_all[par, kv * d:(kv + 1) * d, h * BLOCK:(h + 1) * BLOCK] = (
                q_t[h * d:(h + 1) * d, :].astype(bf16))

        s = jnp.dot(k_aug[par], qt_all[par], preferred_element_type=f32)
        m_list, sink_list = [], []
        for h in range(SWA_HEADS):
            s_h = jnp.where(valid, s[:, h * BLOCK:(h + 1) * BLOCK], NEG_BIG)
            sink_h = (sinks_ref[h] * LOG2E
                      + (slopes_ref[h] * LOG2E) * qpos)
            m_h = jnp.maximum(jnp.max(s_h, axis=0, keepdims=True), sink_h)
            p_scr[par, :, h * BLOCK:(h + 1) * BLOCK] = (
                jnp.exp2(s_h - m_h).astype(bf16))
            m_list.append(m_h)
            sink_list.append(sink_h)

        for kv in range(SWA_KV_HEADS):
            pv = jnp.dot(vt_aug[par, kv],
                         p_scr[par, :, kv * grp * BLOCK:(kv + 1) * grp * BLOCK],
                         preferred_element_type=f32)
            for pair in range(grp // 2):
                halves = []
                for g in (2 * pair, 2 * pair + 1):
                    h = kv * grp + g
                    cols = slice(g * BLOCK, (g + 1) * BLOCK)
                    denom = pv[d:d + 1, cols] + jnp.exp2(
                        sink_list[h] - m_list[h])
                    halves.append(pv[0:d, cols] / denom)
                o_pair = jnp.concatenate(halves, axis=0).T
                c0 = (kv * grp + 2 * pair) * d
                o_ref[r0:r0 + BLOCK, c0:c0 + 2 * d] = o_pair.astype(o_ref.dtype)


def swa_attention(qkv, slopes, sinks, batch, seq_len):
    m = qkv.shape[0]
    nt = seq_len // SWA_TILE
    per_tile = SWA_TILE // BLOCK
    hq = SWA_HEADS * SWA_HEAD_DIM
    kv_w = SWA_KV_HEADS * SWA_HEAD_DIM
    k_col = hq // kv_w
    v_col = k_col + 1
    ncol = SWA_HEADS * BLOCK

    def cur(col):
        return lambda b, t: (b * nt + t, col)

    def prev(col):
        return lambda b, t: (jnp.maximum((b * nt + t) * per_tile - 1, 0), col)

    smem = pl.BlockSpec(memory_space=pltpu.SMEM)
    return pl.pallas_call(
        _swa_kernel,
        grid=(batch, nt),
        in_specs=[smem, smem,
                  pl.BlockSpec((SWA_TILE, hq), cur(0)),
                  pl.BlockSpec((SWA_TILE, kv_w), cur(k_col)),
                  pl.BlockSpec((BLOCK, kv_w), prev(k_col)),
                  pl.BlockSpec((SWA_TILE, kv_w), cur(v_col)),
                  pl.BlockSpec((BLOCK, kv_w), prev(v_col))],
        out_specs=pl.BlockSpec((SWA_TILE, hq), cur(0)),
        out_shape=jax.ShapeDtypeStruct((m, hq), jnp.bfloat16),
        scratch_shapes=[pltpu.VMEM((per_tile, 2 * kv_w, ncol), jnp.bfloat16),
                        pltpu.VMEM((per_tile, 2 * BLOCK, 2 * kv_w), jnp.bfloat16),
                        pltpu.VMEM((per_tile, SWA_KV_HEADS, SWA_VROWS, 2 * BLOCK),
                                   jnp.bfloat16),
                        pltpu.VMEM((per_tile, 2 * BLOCK, ncol), jnp.bfloat16)],
        compiler_params=_params(("arbitrary", "arbitrary")),
        name="swa_attention",
    )(slopes, sinks, qkv, qkv, qkv, qkv, qkv)


def _diff_kernel(slopes_ref, lamv_ref, subg_ref, q_ref, k_ref, v_ref, o_ref,
                 kaug, vt, qaug_t, s_tiles, s_pre, *, lambda_init, seq_len):
    hp = pl.program_id(1)
    tq, tk, d = DIFF_TQ, DIFF_TK, DIFF_HEAD_DIM
    hw = 2 * d
    nq = seq_len // tq
    f32, bf16 = jnp.float32, jnp.bfloat16
    heads = range(DIFF_HEADS_PER_STEP)

    def stage_keys_values():
        lane = lax.broadcasted_iota(jnp.int32, (tk, hw), 1)
        row = lax.broadcasted_iota(jnp.int32, (tk, hw), 0)
        for c in range(seq_len // tk):
            r0 = c * tk
            pos = row + r0
            hi = (pos >> 6).astype(f32)
            lo = (pos & 63).astype(f32)
            aux = jnp.where(lane < 3, hi,
                            jnp.where(lane < 6, lo,
                                      jnp.where(lane < 9, 1.0, 0.0)))
            for g in heads:
                cols = slice(g * hw, (g + 1) * hw)
                kaug[g, c, :, 0:hw] = k_ref[r0:r0 + tk, cols]
                kaug[g, c, :, hw:2 * hw] = aux.astype(bf16)
                vt[g, c, 0:hw, :] = (
                    v_ref[r0:r0 + tk, cols].astype(f32).T.astype(bf16))
                vt[g, c, hw:DIFF_VROWS, :] = jnp.ones((DIFF_VROWS - hw, tk), bf16)

    def setup_queries(g, tile):
        q0 = tile * tq
        q_t = (q_ref[q0:q0 + tq, g * hw:(g + 1) * hw].astype(f32)
               * (LOG2E * d ** -0.5)).T
        qrow = lax.broadcasted_iota(jnp.int32, (hw, tq), 0)
        qaug_t[g, 0:hw, 0:tq] = jnp.where(qrow < d, q_t, 0.0).astype(bf16)
        qaug_t[g, 0:hw, tq:2 * tq] = jnp.where(qrow >= d, q_t, 0.0).astype(bf16)
        a_coef = slopes_ref[hp * DIFF_HEADS_PER_STEP + g] * LOG2E
        c_coef = -a_coef * float(q0)
        r = lax.broadcasted_iota(jnp.int32, (16, 2 * tq), 0)
        t = jnp.where(r < 3, a_coef * 64.0,
                      jnp.where(r < 6, a_coef, jnp.where(r < 9, c_coef, 0.0)))
        t0 = t.astype(bf16).astype(f32)
        t1 = (t - t0).astype(bf16).astype(f32)
        t2 = t - t0 - t1
        first = (r == 0) | (r == 3) | (r == 6)
        second = (r == 1) | (r == 4) | (r == 7)
        qaug_t[g, hw:hw + 16, :] = jnp.where(
            first, t0, jnp.where(second, t1, t2)).astype(bf16)
        qaug_t[g, hw + 16:2 * hw, :] = jnp.zeros((hw - 16, 2 * tq), bf16)

    def scores(j, buf, slot):
        for g in heads:
            s = jnp.dot(kaug[g, j], qaug_t[g], preferred_element_type=f32)
            buf[g, slot, 0:tk, :] = s
            buf[g, slot, tk:tk + 1, :] = jnp.max(s, axis=0, keepdims=True)

    def softmax_pv(j, buf, slot, masked, m_old, acc):
        m_out, acc_out = [], []
        for g in heads:
            s = buf[g, slot, 0:tk, :]
            if masked:
                key = lax.broadcasted_iota(jnp.int32, (tk, 2 * tq), 0)
                qry = lax.broadcasted_iota(jnp.int32, (tk, 2 * tq), 1)
                qry = jnp.where(qry >= tq, qry - tq, qry)
                s = jnp.where(key <= qry, s, NEG_BIG)
                col_max = jnp.max(s, axis=0, keepdims=True)
            else:
                col_max = buf[g, slot, tk:tk + 1, :]
            first = m_old[g] is None
            m_new = col_max if first else jnp.maximum(m_old[g], col_max)
            p = jnp.exp2(s - m_new).astype(bf16)
            pv = jnp.dot(vt[g, j], p, preferred_element_type=f32)
            if first:
                acc_out.append(pv)
            else:
                acc_out.append(acc[g] * jnp.exp2(m_old[g] - m_new) + pv)
            m_out.append(m_new)
        return m_out, acc_out

    def normalise_and_store(qi, acc_all):
        lam = (jnp.exp(jnp.sum(lamv_ref[0:1, :] * lamv_ref[1:2, :]))
               - jnp.exp(jnp.sum(lamv_ref[2:3, :] * lamv_ref[3:4, :]))
               + lambda_init)
        for g in heads:
            acc = acc_all[g]
            o_all = acc[0:hw, :] * (1.0 / acc[hw:hw + 1, :])
            o_t = o_all[:, 0:tq] - lam * o_all[:, tq:2 * tq]
            o_t = o_t * lax.rsqrt(
                jnp.mean(o_t * o_t, axis=0, keepdims=True) + EPS)
            o = o_t.T * (subg_ref[...] * (1.0 - lambda_init))
            o_ref[qi * tq:(qi + 1) * tq,
                  g * hw:(g + 1) * hw] = o.astype(o_ref.dtype)

    def query_tile(c):
        m = [None] * len(heads)
        acc = [None] * len(heads)
        for j in range(c + 1):
            if j < c:
                scores(j + 1, s_tiles, j)
            buf, slot = (s_pre, c % 2) if j == 0 else (s_tiles, j - 1)
            m, acc = softmax_pv(j, buf, slot, j == c, m, acc)
        for g in heads:
            setup_queries(g, min(c + 1, nq - 1))
        scores(0, s_pre, (c + 1) % 2)
        normalise_and_store(c, acc)

    stage_keys_values()
    for g in heads:
        setup_queries(g, 0)
    scores(0, s_pre, 0)

    def tile_step(qi, carry):
        for c in range(nq):
            pl.when(qi == c)(functools.partial(query_tile, c))
        return carry

    lax.fori_loop(0, nq, tile_step, 0)


def diff_attention(qkv, slopes, lam_vecs, subln_g, lambda_init, batch, seq_len):
    m = qkv.shape[0]
    hw = 2 * DIFF_HEAD_DIM
    nq = seq_len // DIFF_TQ
    smem = pl.BlockSpec(memory_space=pltpu.SMEM)
    assert DIFF_TQ == DIFF_TK and seq_len % DIFF_TK == 0
    nk = seq_len // DIFF_TK
    hps = DIFF_HEADS_PER_STEP
    groups = DIFF_HEADS // hps
    bw = hps * hw
    score_tile = (DIFF_TK + 8, 2 * DIFF_TQ)
    return pl.pallas_call(
        functools.partial(_diff_kernel, lambda_init=lambda_init,
                          seq_len=seq_len),
        grid=(batch, groups),
        in_specs=[smem,
                  _const_spec(lam_vecs.shape),
                  _const_spec((1, hw)),
                  pl.BlockSpec((seq_len, bw), lambda b, h: (b, h)),
                  pl.BlockSpec((seq_len, bw), lambda b, h: (b, groups + h)),
                  pl.BlockSpec((seq_len, bw), lambda b, h: (b, 2 * groups + h))],
        out_specs=pl.BlockSpec((seq_len, bw), lambda b, h: (b, h)),
        out_shape=jax.ShapeDtypeStruct((m, DIFF_HEADS * hw), jnp.bfloat16),
        scratch_shapes=[pltpu.VMEM((hps, nk, DIFF_TK, 2 * hw), jnp.bfloat16),
                        pltpu.VMEM((hps, nk, DIFF_VROWS, DIFF_TK), jnp.bfloat16),
                        pltpu.VMEM((hps, 2 * hw, 2 * DIFF_TQ), jnp.bfloat16),
                        pltpu.VMEM((hps, nk - 1) + score_tile, jnp.float32),
                        pltpu.VMEM((hps, 2) + score_tile, jnp.float32)],
        compiler_params=_params(("parallel", "parallel")),
        name="diff_attention",
    )(slopes, lam_vecs, subln_g.reshape(1, hw), qkv, qkv, qkv)


def _alibi_slopes(n_heads):
    return np.exp2(-8.0 * (np.arange(n_heads, dtype=np.float32) + 1.0)
                   / n_heads).astype(np.float32)


def _diff_lambda_init(layer_idx):
    return 0.8 - 0.6 * math.exp(-0.3 * layer_idx)


def kernel(x, mix_pre_g, mix_post_g, ffn_pre_g, ffn_post_g, swa_w_qkv, swa_sinks, swa_w_o, diff_w_qkv, diff_lam_q1, diff_lam_k1, diff_lam_q2, diff_lam_k2, diff_subln_g, diff_w_o, ffn_w_up, ffn_conv_w, ffn_conv_b, ffn_w_down):
    batch, seq_len, d = x.shape
    depth = mix_pre_g.shape[0]
    bf = jnp.bfloat16
    swa_w_qkv, swa_w_o = swa_w_qkv.astype(bf), swa_w_o.astype(bf)
    diff_w_qkv, diff_w_o = diff_w_qkv.astype(bf), diff_w_o.astype(bf)
    ffn_w_up, ffn_w_down = ffn_w_up.astype(bf), ffn_w_down.astype(bf)
    x2d = x.reshape(batch * seq_len, d)
    for i in range(depth):
        j = i // N_MIXERS
        if i % N_MIXERS == 0:
            qkv = norm_matmul(x2d, mix_pre_g[i], swa_w_qkv, j, n_chunk=640)
            a = swa_attention(qkv, _alibi_slopes(SWA_HEADS),
                              swa_sinks[j].astype(jnp.float32), batch, seq_len)
            w_o = swa_w_o
        else:
            qkv = norm_matmul(x2d, mix_pre_g[i], diff_w_qkv, j)
            lam_vecs = jnp.stack([diff_lam_q1[j], diff_lam_k1[j],
                                  diff_lam_q2[j], diff_lam_k2[j]]).astype(jnp.float32)
            a = diff_attention(qkv, _alibi_slopes(DIFF_HEADS), lam_vecs,
                               diff_subln_g[j], _diff_lambda_init(i),
                               batch, seq_len)
            w_o = diff_w_o
        x2d, act = mixer_out_ffn_up(a, w_o, j, mix_post_g[i], x2d, ffn_pre_g[i],
                                    ffn_w_up, ffn_conv_w, i, ffn_conv_b[i],
                                    seq_len)
        x2d = matmul_norm_residual(act, ffn_w_down, i, ffn_post_g[i], x2d)
    return x2d.reshape(batch, seq_len, d)
```

```python
import functools
import math

import jax
import jax.numpy as jnp
import numpy as np
from jax import lax
from jax.experimental import pallas as pl
from jax.experimental.pallas import tpu as pltpu

D_MODEL = 1024
EPS = 1e-6
BLOCK = 128
SWA_HEADS = 16
SWA_KV_HEADS = 2
SWA_HEAD_DIM = 64
SWA_GROUP = SWA_HEADS // SWA_KV_HEADS
DIFF_HEADS = 8
DIFF_HEAD_DIM = 64
D_FF = 2816
CONV_WIDTH = 3
N_MIXERS = 2

VMEM_LIMIT_BYTES = 56 * 1024 * 1024
LANES = 128
HALO = 16
GELU_C0 = math.sqrt(2.0 / math.pi)
GELU_C1 = 0.044715 * GELU_C0
NEG_BIG = -1e30

ROW_TILE = 1024
FFN_ROW_TILE = 1024
FF_CHUNK = 256
SWA_TILE = 1024
SWA_VROWS = 80
DIFF_TQ = 512
DIFF_TK = 512
DIFF_VROWS = 144
DIFF_HEADS_PER_STEP = 1
LOG2E = 1.4426950408889634


def _rms(x, g):
    return x * lax.rsqrt(jnp.mean(x * x, axis=-1, keepdims=True) + EPS) * g


def _params(sem):
    return pltpu.CompilerParams(dimension_semantics=sem,
                                vmem_limit_bytes=VMEM_LIMIT_BYTES)


def _layer_spec(shape, layer):
    nd = len(shape)
    return pl.BlockSpec((None,) + tuple(shape[1:]),
                        lambda *_: (layer,) + (0,) * (nd - 1),
                        pipeline_mode=pl.Buffered(1))


def _const_spec(shape):
    nd = len(shape)
    return pl.BlockSpec(shape, lambda *_: (0,) * nd,
                        pipeline_mode=pl.Buffered(1))


def _norm_matmul_kernel(x_ref, g_ref, w_ref, o_ref, *, n_chunk):
    h = _rms(x_ref[...], g_ref[...]).astype(jnp.bfloat16)
    n = o_ref.shape[1]
    for c in range(0, n, n_chunk):
        o_ref[:, c:c + n_chunk] = jnp.dot(
            h, w_ref[:, c:c + n_chunk],
            preferred_element_type=jnp.float32).astype(o_ref.dtype)


def norm_matmul(x2d, g, w_stack, layer, n_chunk=512):
    m, d = x2d.shape
    n = w_stack.shape[2]
    return pl.pallas_call(
        functools.partial(_norm_matmul_kernel, n_chunk=n_chunk),
        grid=(m // ROW_TILE,),
        in_specs=[pl.BlockSpec((ROW_TILE, d), lambda i: (i, 0)),
                  _const_spec((1, d)),
                  _layer_spec(w_stack.shape, layer)],
        out_specs=pl.BlockSpec((ROW_TILE, n), lambda i: (i, 0)),
        out_shape=jax.ShapeDtypeStruct((m, n), jnp.bfloat16),
        compiler_params=_params(("parallel",)),
        name="norm_matmul",
    )(x2d, g.reshape(1, d), w_stack)


def _matmul_norm_res_kernel(a_ref, w_ref, g_ref, x_ref, o_ref):
    y = jnp.dot(a_ref[...], w_ref[...], preferred_element_type=jnp.float32)
    o_ref[...] = x_ref[...] + _rms(y, g_ref[...])


def matmul_norm_residual(a_bf16, w_stack, layer, g, x2d):
    m, k = a_bf16.shape
    d = w_stack.shape[2]
    return pl.pallas_call(
        _matmul_norm_res_kernel,
        grid=(m // ROW_TILE,),
        in_specs=[pl.BlockSpec((ROW_TILE, k), lambda i: (i, 0)),
                  _layer_spec(w_stack.shape, layer),
                  _const_spec((1, d)),
                  pl.BlockSpec((ROW_TILE, d), lambda i: (i, 0))],
        out_specs=pl.BlockSpec((ROW_TILE, d), lambda i: (i, 0)),
        out_shape=jax.ShapeDtypeStruct((m, d), jnp.float32),
        compiler_params=_params(("parallel",)),
        name="matmul_norm_residual",
    )(a_bf16, w_stack, g.reshape(1, d), x2d)


def _ffn_up_kernel(ahalo_ref, a_ref, wo_ref, gpost_ref, halo_ref, x_ref, g_ref,
                   wup_ref, cw_ref, cb_ref, x1_ref, o_ref,
                   h_scr, ug_scr, uv_scr, *, seq_len):
    i = pl.program_id(0)
    tm = x_ref.shape[0]
    a_ext = jnp.concatenate([ahalo_ref[...], a_ref[...]], axis=0)
    y = jnp.dot(a_ext, wo_ref[...], preferred_element_type=jnp.float32)
    x_ext = jnp.concatenate([halo_ref[...], x_ref[...]], axis=0)
    x1 = x_ext + _rms(y, gpost_ref[...])
    x1_ref[...] = x1[HALO:, :]
    g = g_ref[...]
    first = (i * tm) % seq_len == 0
    h_scr[0:HALO, :] = _rms(jnp.where(first, 0.0, x1[0:HALO, :]),
                            g).astype(jnp.bfloat16)
    h_scr[HALO:, :] = _rms(x1[HALO:, :], g).astype(jnp.bfloat16)

    def conv(u_scr, slab, col):
        c = cb_ref[:, col:col + LANES]
        for tap in range(CONV_WIDTH):
            off = HALO - (CONV_WIDTH - 1) + tap
            rows = pl.ds(off, tm) if off % 8 == 0 else pl.ds(off, tm, stride=1)
            c = c + cw_ref[tap:tap + 1, col:col + LANES] * u_scr[slab, rows, :]
        return c

    slabs = FF_CHUNK // LANES
    for j in range(D_FF // FF_CHUNK):
        cg = j * FF_CHUNK
        cv = D_FF + cg
        ug = jnp.dot(h_scr[...], wup_ref[:, cg:cg + FF_CHUNK],
                     preferred_element_type=jnp.float32)
        uv = jnp.dot(h_scr[...], wup_ref[:, cv:cv + FF_CHUNK],
                     preferred_element_type=jnp.float32)
        base = (j % 2) * slabs
        for s in range(slabs):
            ug_scr[base + s] = ug[:, s * LANES:(s + 1) * LANES]
            uv_scr[base + s] = uv[:, s * LANES:(s + 1) * LANES]
        for s in range(slabs):
            gate = conv(ug_scr, base + s, cg + s * LANES)
            val = conv(uv_scr, base + s, cv + s * LANES)
            t = jnp.tanh(gate * (GELU_C0 + GELU_C1 * (gate * gate)))
            o_ref[:, cg + s * LANES:cg + (s + 1) * LANES] = (
                (gate * val) * (0.5 * t + 0.5)).astype(o_ref.dtype)


def mixer_out_ffn_up(a_bf16, wo_stack, wo_layer, g_post, x2d, g, wup_stack,
                     conv_w_stack, layer, conv_b, seq_len):
    m, d = x2d.shape
    ka = a_bf16.shape[1]
    n2 = wup_stack.shape[2]
    tm = FFN_ROW_TILE
    halo_blocks = tm // HALO
    u_scratch = pltpu.VMEM((2 * FF_CHUNK // LANES, tm + HALO, LANES), jnp.float32)

    def halo_map(i):
        return (jnp.maximum(i * halo_blocks - 1, 0), 0)

    def tile_map(i):
        return (i, 0)

    return pl.pallas_call(
        functools.partial(_ffn_up_kernel, seq_len=seq_len),
        grid=(m // tm,),
        in_specs=[pl.BlockSpec((HALO, ka), halo_map),
                  pl.BlockSpec((tm, ka), tile_map),
                  _layer_spec(wo_stack.shape, wo_layer),
                  _const_spec((1, d)),
                  pl.BlockSpec((HALO, d), halo_map),
                  pl.BlockSpec((tm, d), tile_map),
                  _const_spec((1, d)),
                  _layer_spec(wup_stack.shape, layer),
                  _layer_spec(conv_w_stack.shape, layer),
                  _const_spec((1, n2))],
        out_specs=[pl.BlockSpec((tm, d), tile_map),
                   pl.BlockSpec((tm, D_FF), tile_map)],
        out_shape=[jax.ShapeDtypeStruct((m, d), jnp.float32),
                   jax.ShapeDtypeStruct((m, D_FF), jnp.bfloat16)],
        scratch_shapes=[pltpu.VMEM((tm + HALO, d), jnp.bfloat16),
                        u_scratch, u_scratch],
        compiler_params=_params(("parallel",)),
        name="mixer_out_ffn_up",
    )(a_bf16, a_bf16, wo_stack, g_post.reshape(1, d), x2d, x2d, g.reshape(1, d),
      wup_stack, conv_w_stack, conv_b.reshape(1, n2))


def _swa_kernel(slopes_ref, sinks_ref, q_ref, k_ref, kp_ref, v_ref, vp_ref,
                o_ref, qt_all, k_aug, vt_aug, p_scr):
    t = pl.program_id(1)
    f32, bf16 = jnp.float32, jnp.bfloat16
    d, grp = SWA_HEAD_DIM, SWA_GROUP
    ncol = SWA_HEADS * BLOCK
    kvw = SWA_KV_HEADS * d

    @pl.when((pl.program_id(0) == 0) & (t == 0))
    def _init_constants():
        colh = lax.broadcasted_iota(jnp.int32, (16, ncol), 1) >> 7
        r = lax.broadcasted_iota(jnp.int32, (16, ncol), 0)
        a = jnp.zeros((16, ncol), f32)
        for h in range(SWA_HEADS):
            a = jnp.where(colh == h, slopes_ref[h] * LOG2E, a)
        a0 = a.astype(bf16).astype(f32)
        a1 = (a - a0).astype(bf16).astype(f32)
        a2 = a - a0 - a1
        coef = jnp.where(r == 0, a0, jnp.where(r == 1, a1,
                                               jnp.where(r == 2, a2, 0.0)))
        kj = lax.broadcasted_iota(jnp.int32, (2 * BLOCK, kvw), 0).astype(f32)
        lane = lax.broadcasted_iota(jnp.int32, (2 * BLOCK, kvw), 1)
        for par in range(SWA_TILE // BLOCK):
            qt_all[par] = jnp.zeros((2 * kvw, ncol), bf16)
            qt_all[par, kvw:kvw + 16, :] = coef.astype(bf16)
            k_aug[par, :, kvw:2 * kvw] = jnp.where(lane < 3, kj, 0.0).astype(bf16)
            for kv in range(SWA_KV_HEADS):
                vt_aug[par, kv, d:SWA_VROWS, :] = jnp.ones(
                    (SWA_VROWS - d, 2 * BLOCK), bf16)

    key = lax.broadcasted_iota(jnp.int32, (2 * BLOCK, BLOCK), 0)
    qry = lax.broadcasted_iota(jnp.int32, (2 * BLOCK, BLOCK), 1)
    dist = qry + BLOCK - key
    band = (dist >= 0) & (dist < BLOCK)
    band_first = band & (key >= jnp.where(t > 0, 0, BLOCK))
    qpos = (lax.broadcasted_iota(jnp.int32, (1, BLOCK), 1) + BLOCK).astype(f32)

    for blk in range(SWA_TILE // BLOCK):
        par = blk
        r0 = blk * BLOCK
        valid = band_first if blk == 0 else band
        k_prev = kp_ref[...] if blk == 0 else k_ref[r0 - BLOCK:r0, :]
        v_prev = vp_ref[...] if blk == 0 else v_ref[r0 - BLOCK:r0, :]
        k_aug[par, 0:BLOCK, 0:kvw] = k_prev
        k_aug[par, BLOCK:2 * BLOCK, 0:kvw] = k_ref[r0:r0 + BLOCK, :]
        v_t = jnp.concatenate([v_prev, v_ref[r0:r0 + BLOCK, :]],
                              axis=0).astype(f32).T
        for kv in range(SWA_KV_HEADS):
            vt_aug[par, kv, 0:d, :] = v_t[kv * d:(kv + 1) * d, :].astype(bf16)
        q_t = (q_ref[r0:r0 + BLOCK, :].astype(f32) * (LOG2E * d ** -0.5)).T
        for h in range(SWA_HEADS):
            kv = h // grp
            qt_all[par, kv * d:(kv + 1) * d, h * BLOCK:(h + 1) * BLOCK] = (
                q_t[h * d:(h + 1) * d, :].astype(bf16))

        s = jnp.dot(k_aug[par], qt_all[par], preferred_element_type=f32)
        m_list, sink_list = [], []
        for h in range(SWA_HEADS):
            s_h = jnp.where(valid, s[:, h * BLOCK:(h + 1) * BLOCK], NEG_BIG)
            sink_h = (sinks_ref[h] * LOG2E
                      + (slopes_ref[h] * LOG2E) * qpos)
            m_h = jnp.maximum(jnp.max(s_h, axis=0, keepdims=True), sink_h)
            p_scr[par, :, h * BLOCK:(h + 1) * BLOCK] = (
                jnp.exp2(s_h - m_h).astype(bf16))
            m_list.append(m_h)
            sink_list.append(sink_h)

        for kv in range(SWA_KV_HEADS):
            pv = jnp.dot(vt_aug[par, kv],
                         p_scr[par, :, kv * grp * BLOCK:(kv + 1) * grp * BLOCK],
                         preferred_element_type=f32)
            for pair in range(grp // 2):
                halves = []
                for g in (2 * pair, 2 * pair + 1):
                    h = kv * grp + g
                    cols = slice(g * BLOCK, (g + 1) * BLOCK)
                    denom = pv[d:d + 1, cols] + jnp.exp2(
                        sink_list[h] - m_list[h])
                    halves.append(pv[0:d, cols] / denom)
                o_pair = jnp.concatenate(halves, axis=0).T
                c0 = (kv * grp + 2 * pair) * d
                o_ref[r0:r0 + BLOCK, c0:c0 + 2 * d] = o_pair.astype(o_ref.dtype)


def swa_attention(qkv, slopes, sinks, batch, seq_len):
    m = qkv.shape[0]
    nt = seq_len // SWA_TILE
    per_tile = SWA_TILE // BLOCK
    hq = SWA_HEADS * SWA_HEAD_DIM
    kv_w = SWA_KV_HEADS * SWA_HEAD_DIM
    k_col = hq // kv_w
    v_col = k_col + 1
    ncol = SWA_HEADS * BLOCK

    def cur(col):
        return lambda b, t: (b * nt + t, col)

    def prev(col):
        return lambda b, t: (jnp.maximum((b * nt + t) * per_tile - 1, 0), col)

    smem = pl.BlockSpec(memory_space=pltpu.SMEM)
    return pl.pallas_call(
        _swa_kernel,
        grid=(batch, nt),
        in_specs=[smem, smem,
                  pl.BlockSpec((SWA_TILE, hq), cur(0)),
                  pl.BlockSpec((SWA_TILE, kv_w), cur(k_col)),
                  pl.BlockSpec((BLOCK, kv_w), prev(k_col)),
                  pl.BlockSpec((SWA_TILE, kv_w), cur(v_col)),
                  pl.BlockSpec((BLOCK, kv_w), prev(v_col))],
        out_specs=pl.BlockSpec((SWA_TILE, hq), cur(0)),
        out_shape=jax.ShapeDtypeStruct((m, hq), jnp.bfloat16),
        scratch_shapes=[pltpu.VMEM((per_tile, 2 * kv_w, ncol), jnp.bfloat16),
                        pltpu.VMEM((per_tile, 2 * BLOCK, 2 * kv_w), jnp.bfloat16),
                        pltpu.VMEM((per_tile, SWA_KV_HEADS, SWA_VROWS, 2 * BLOCK),
                                   jnp.bfloat16),
                        pltpu.VMEM((per_tile, 2 * BLOCK, ncol), jnp.bfloat16)],
        compiler_params=_params(("arbitrary", "arbitrary")),
        name="swa_attention",
    )(slopes, sinks, qkv, qkv, qkv, qkv, qkv)


def _diff_kernel(slopes_ref, lamv_ref, subg_ref, q_ref, k_ref, v_ref, o_ref,
                 kaug, vt, qaug_t, s_tiles, s_pre, *, lambda_init, seq_len):
    hp = pl.program_id(1)
    tq, tk, d = DIFF_TQ, DIFF_TK, DIFF_HEAD_DIM
    hw = 2 * d
    nq = seq_len // tq
    f32, bf16 = jnp.float32, jnp.bfloat16
    heads = range(DIFF_HEADS_PER_STEP)

    def stage_keys_values():
        lane = lax.broadcasted_iota(jnp.int32, (tk, hw), 1)
        row = lax.broadcasted_iota(jnp.int32, (tk, hw), 0)
        for c in range(seq_len // tk):
            r0 = c * tk
            pos = row + r0
            hi = (pos >> 6).astype(f32)
            lo = (pos & 63).astype(f32)
            aux = jnp.where(lane < 3, hi,
                            jnp.where(lane < 6, lo,
                                      jnp.where(lane < 9, 1.0, 0.0)))
            for g in heads:
                cols = slice(g * hw, (g + 1) * hw)
                kaug[g, c, :, 0:hw] = k_ref[r0:r0 + tk, cols]
                kaug[g, c, :, hw:2 * hw] = aux.astype(bf16)
                vt[g, c, 0:hw, :] = (
                    v_ref[r0:r0 + tk, cols].astype(f32).T.astype(bf16))
                vt[g, c, hw:DIFF_VROWS, :] = jnp.ones((DIFF_VROWS - hw, tk), bf16)

    def setup_queries(g, tile):
        q0 = tile * tq
        q_t = (q_ref[q0:q0 + tq, g * hw:(g + 1) * hw].astype(f32)
               * (LOG2E * d ** -0.5)).T
        qrow = lax.broadcasted_iota(jnp.int32, (hw, tq), 0)
        qaug_t[g, 0:hw, 0:tq] = jnp.where(qrow < d, q_t, 0.0).astype(bf16)
        qaug_t[g, 0:hw, tq:2 * tq] = jnp.where(qrow >= d, q_t, 0.0).astype(bf16)
        a_coef = slopes_ref[hp * DIFF_HEADS_PER_STEP + g] * LOG2E
        c_coef = -a_coef * float(q0)
        r = lax.broadcasted_iota(jnp.int32, (16, 2 * tq), 0)
        t = jnp.where(r < 3, a_coef * 64.0,
                      jnp.where(r < 6, a_coef, jnp.where(r < 9, c_coef, 0.0)))
        t0 = t.astype(bf16).astype(f32)
        t1 = (t - t0).astype(bf16).astype(f32)
        t2 = t - t0 - t1
        first = (r == 0) | (r == 3) | (r == 6)
        second = (r == 1) | (r == 4) | (r == 7)
        qaug_t[g, hw:hw + 16, :] = jnp.where(
            first, t0, jnp.where(second, t1, t2)).astype(bf16)
        qaug_t[g, hw + 16:2 * hw, :] = jnp.zeros((hw - 16, 2 * tq), bf16)

    half = tk // 2
    late_cols = (slice(half, tq), slice(tq + half, 2 * tq))

    def scores(j, buf, slot, diagonal=False):
        for g in heads:
            if diagonal:
                buf[g, slot, 0:half, :] = jnp.dot(
                    kaug[g, j, 0:half, :], qaug_t[g], preferred_element_type=f32)
                for cols in late_cols:
                    buf[g, slot, half:tk, cols] = jnp.dot(
                        kaug[g, j, half:tk, :], qaug_t[g, :, cols],
                        preferred_element_type=f32)
            else:
                s = jnp.dot(kaug[g, j], qaug_t[g], preferred_element_type=f32)
                buf[g, slot, 0:tk, :] = s
                buf[g, slot, tk:tk + 1, :] = jnp.max(s, axis=0, keepdims=True)

    def accumulate(m_old, acc, m_new, pv):
        if m_old is None:
            return pv
        return acc * jnp.exp2(m_old - m_new) + pv

    def softmax_pv(j, buf, slot, m_old, acc):
        m_out, acc_out = [], []
        for g in heads:
            s = buf[g, slot, 0:tk, :]
            col_max = buf[g, slot, tk:tk + 1, :]
            m_new = (col_max if m_old[g] is None
                     else jnp.maximum(m_old[g], col_max))
            p = jnp.exp2(s - m_new).astype(bf16)
            pv = jnp.dot(vt[g, j], p, preferred_element_type=f32)
            acc_out.append(accumulate(m_old[g], acc[g], m_new, pv))
            m_out.append(m_new)
        return m_out, acc_out

    def softmax_pv_diagonal(j, buf, slot, m_old, acc):
        m_out, acc_out = [], []
        row = lax.broadcasted_iota(jnp.int32, (half, half), 0)
        col = lax.broadcasted_iota(jnp.int32, (half, half), 1)
        tri = row <= col
        for g in heads:
            top, bot, col_max = [], [], []
            for b in range(2 * tq // half):
                cols = slice(b * half, (b + 1) * half)
                s_top = buf[g, slot, 0:half, cols]
                if b % 2 == 0:
                    s_top = jnp.where(tri, s_top, NEG_BIG)
                    s_bot = None
                    cm = jnp.max(s_top, axis=0, keepdims=True)
                else:
                    s_bot = jnp.where(tri, buf[g, slot, half:tk, cols], NEG_BIG)
                    cm = jnp.maximum(jnp.max(s_top, axis=0, keepdims=True),
                                     jnp.max(s_bot, axis=0, keepdims=True))
                top.append(s_top)
                bot.append(s_bot)
                col_max.append(cm)
            col_max = jnp.concatenate(col_max, axis=1)
            m_new = (col_max if m_old[g] is None
                     else jnp.maximum(m_old[g], col_max))
            p_top = jnp.concatenate(
                [jnp.exp2(s_b - m_new[:, b * half:(b + 1) * half]).astype(bf16)
                 for b, s_b in enumerate(top)], axis=1)
            pv = jnp.dot(vt[g, j, :, 0:half], p_top, preferred_element_type=f32)
            parts = []
            for b, s_b in enumerate(bot):
                cols = slice(b * half, (b + 1) * half)
                if s_b is None:
                    parts.append(pv[:, cols])
                else:
                    p_b = jnp.exp2(s_b - m_new[:, cols]).astype(bf16)
                    parts.append(pv[:, cols] + jnp.dot(
                        vt[g, j, :, half:tk], p_b, preferred_element_type=f32))
            pv = jnp.concatenate(parts, axis=1)
            acc_out.append(accumulate(m_old[g], acc[g], m_new, pv))
            m_out.append(m_new)
        return m_out, acc_out

    def normalise_and_store(qi, acc_all):
        lam = (jnp.exp(jnp.sum(lamv_ref[0:1, :] * lamv_ref[1:2, :]))
               - jnp.exp(jnp.sum(lamv_ref[2:3, :] * lamv_ref[3:4, :]))
               + lambda_init)
        for g in heads:
            acc = acc_all[g]
            o_all = acc[0:hw, :] * (1.0 / acc[hw:hw + 1, :])
            o_t = o_all[:, 0:tq] - lam * o_all[:, tq:2 * tq]
            o_t = o_t * lax.rsqrt(
                jnp.mean(o_t * o_t, axis=0, keepdims=True) + EPS)
            o = o_t.T * (subg_ref[...] * (1.0 - lambda_init))
            o_ref[qi * tq:(qi + 1) * tq,
                  g * hw:(g + 1) * hw] = o.astype(o_ref.dtype)

    def query_tile(c):
        m = [None] * len(heads)
        acc = [None] * len(heads)
        for j in range(c + 1):
            if j < c:
                scores(j + 1, s_tiles, j, diagonal=(j + 1 == c))
            buf, slot = (s_pre, c % 2) if j == 0 else (s_tiles, j - 1)
            step = softmax_pv_diagonal if j == c else softmax_pv
            m, acc = step(j, buf, slot, m, acc)
        for g in heads:
            setup_queries(g, min(c + 1, nq - 1))
        scores(0, s_pre, (c + 1) % 2)
        normalise_and_store(c, acc)

    stage_keys_values()
    for g in heads:
        setup_queries(g, 0)
    scores(0, s_pre, 0, diagonal=True)

    def tile_step(qi, carry):
        for c in range(nq):
            pl.when(qi == c)(functools.partial(query_tile, c))
        return carry

    lax.fori_loop(0, nq, tile_step, 0)


def diff_attention(qkv, slopes, lam_vecs, subln_g, lambda_init, batch, seq_len):
    m = qkv.shape[0]
    hw = 2 * DIFF_HEAD_DIM
    nq = seq_len // DIFF_TQ
    smem = pl.BlockSpec(memory_space=pltpu.SMEM)
    assert DIFF_TQ == DIFF_TK and seq_len % DIFF_TK == 0
    nk = seq_len // DIFF_TK
    hps = DIFF_HEADS_PER_STEP
    groups = DIFF_HEADS // hps
    bw = hps * hw
    score_tile = (DIFF_TK + 8, 2 * DIFF_TQ)
    return pl.pallas_call(
        functools.partial(_diff_kernel, lambda_init=lambda_init,
                          seq_len=seq_len),
        grid=(batch, groups),
        in_specs=[smem,
                  _const_spec(lam_vecs.shape),
                  _const_spec((1, hw)),
                  pl.BlockSpec((seq_len, bw), lambda b, h: (b, h)),
                  pl.BlockSpec((seq_len, bw), lambda b, h: (b, groups + h)),
                  pl.BlockSpec((seq_len, bw), lambda b, h: (b, 2 * groups + h))],
        out_specs=pl.BlockSpec((seq_len, bw), lambda b, h: (b, h)),
        out_shape=jax.ShapeDtypeStruct((m, DIFF_HEADS * hw), jnp.bfloat16),
        scratch_shapes=[pltpu.VMEM((hps, nk, DIFF_TK, 2 * hw), jnp.bfloat16),
                        pltpu.VMEM((hps, nk, DIFF_VROWS, DIFF_TK), jnp.bfloat16),
                        pltpu.VMEM((hps, 2 * hw, 2 * DIFF_TQ), jnp.bfloat16),
                        pltpu.VMEM((hps, nk - 1) + score_tile, jnp.float32),
                        pltpu.VMEM((hps, 2) + score_tile, jnp.float32)],
        compiler_params=_params(("parallel", "parallel")),
        name="diff_attention",
    )(slopes, lam_vecs, subln_g.reshape(1, hw), qkv, qkv, qkv)


def _alibi_slopes(n_heads):
    return np.exp2(-8.0 * (np.arange(n_heads, dtype=np.float32) + 1.0)
                   / n_heads).astype(np.float32)


def _diff_lambda_init(layer_idx):
    return 0.8 - 0.6 * math.exp(-0.3 * layer_idx)


def kernel(x, mix_pre_g, mix_post_g, ffn_pre_g, ffn_post_g, swa_w_qkv, swa_sinks, swa_w_o, diff_w_qkv, diff_lam_q1, diff_lam_k1, diff_lam_q2, diff_lam_k2, diff_subln_g, diff_w_o, ffn_w_up, ffn_conv_w, ffn_conv_b, ffn_w_down):
    batch, seq_len, d = x.shape
    depth = mix_pre_g.shape[0]
    bf = jnp.bfloat16
    swa_w_qkv, swa_w_o = swa_w_qkv.astype(bf), swa_w_o.astype(bf)
    diff_w_qkv, diff_w_o = diff_w_qkv.astype(bf), diff_w_o.astype(bf)
    ffn_w_up, ffn_w_down = ffn_w_up.astype(bf), ffn_w_down.astype(bf)
    x2d = x.reshape(batch * seq_len, d)
    for i in range(depth):
        j = i // N_MIXERS
        if i % N_MIXERS == 0:
            qkv = norm_matmul(x2d, mix_pre_g[i], swa_w_qkv, j, n_chunk=640)
            a = swa_attention(qkv, _alibi_slopes(SWA_HEADS),
                              swa_sinks[j].astype(jnp.float32), batch, seq_len)
            w_o = swa_w_o
        else:
            qkv = norm_matmul(x2d, mix_pre_g[i], diff_w_qkv, j)
            lam_vecs = jnp.stack([diff_lam_q1[j], diff_lam_k1[j],
                                  diff_lam_q2[j], diff_lam_k2[j]]).astype(jnp.float32)
            a = diff_attention(qkv, _alibi_slopes(DIFF_HEADS), lam_vecs,
                               diff_subln_g[j], _diff_lambda_init(i),
                               batch, seq_len)
            w_o = diff_w_o
        x2d, act = mixer_out_ffn_up(a, w_o, j, mix_post_g[i], x2d, ffn_pre_g[i],
                                    ffn_w_up, ffn_conv_w, i, ffn_conv_b[i],
                                    seq_len)
        x2d = matmul_norm_residual(act, ffn_w_down, i, ffn_post_g[i], x2d)
    return x2d.reshape(batch, seq_len, d)
```

```python
import functools
import math

import jax
import jax.numpy as jnp
import numpy as np
from jax import lax
from jax.experimental import pallas as pl
from jax.experimental.pallas import tpu as pltpu

D_MODEL = 1024
EPS = 1e-6
BLOCK = 128
SWA_HEADS = 16
SWA_KV_HEADS = 2
SWA_HEAD_DIM = 64
SWA_GROUP = SWA_HEADS // SWA_KV_HEADS
DIFF_HEADS = 8
DIFF_HEAD_DIM = 64
D_FF = 2816
CONV_WIDTH = 3
N_MIXERS = 2

VMEM_LIMIT_BYTES = 56 * 1024 * 1024
LANES = 128
HALO = 16
GELU_C0 = math.sqrt(2.0 / math.pi)
GELU_C1 = 0.044715 * GELU_C0
NEG_BIG = -1e30

ROW_TILE = 1024
FFN_ROW_TILE = 1024
FF_CHUNK = 256
SWA_TILE = 1024
SWA_VROWS = 80
DIFF_TQ = 512
DIFF_TK = 512
DIFF_VROWS = 144
DIFF_HEADS_PER_STEP = 1
LOG2E = 1.4426950408889634


def _rms(x, g):
    return x * lax.rsqrt(jnp.mean(x * x, axis=-1, keepdims=True) + EPS) * g


def _params(sem):
    return pltpu.CompilerParams(dimension_semantics=sem,
                                vmem_limit_bytes=VMEM_LIMIT_BYTES)


def _layer_spec(shape, layer):
    nd = len(shape)
    return pl.BlockSpec((None,) + tuple(shape[1:]),
                        lambda *_: (layer,) + (0,) * (nd - 1),
                        pipeline_mode=pl.Buffered(1))


def _const_spec(shape):
    nd = len(shape)
    return pl.BlockSpec(shape, lambda *_: (0,) * nd,
                        pipeline_mode=pl.Buffered(1))


def _norm_matmul_kernel(x_ref, g_ref, w_ref, o_ref, *, n_chunk):
    h = _rms(x_ref[...], g_ref[...]).astype(jnp.bfloat16)
    n = o_ref.shape[1]
    for c in range(0, n, n_chunk):
        o_ref[:, c:c + n_chunk] = jnp.dot(
            h, w_ref[:, c:c + n_chunk],
            preferred_element_type=jnp.float32).astype(o_ref.dtype)


def norm_matmul(x2d, g, w_stack, layer, n_chunk=512):
    m, d = x2d.shape
    n = w_stack.shape[2]
    return pl.pallas_call(
        functools.partial(_norm_matmul_kernel, n_chunk=n_chunk),
        grid=(m // ROW_TILE,),
        in_specs=[pl.BlockSpec((ROW_TILE, d), lambda i: (i, 0)),
                  _const_spec((1, d)),
                  _layer_spec(w_stack.shape, layer)],
        out_specs=pl.BlockSpec((ROW_TILE, n), lambda i: (i, 0)),
        out_shape=jax.ShapeDtypeStruct((m, n), jnp.bfloat16),
        compiler_params=_params(("parallel",)),
        name="norm_matmul",
    )(x2d, g.reshape(1, d), w_stack)


def _matmul_norm_res_kernel(a_ref, w_ref, g_ref, x_ref, o_ref):
    y = jnp.dot(a_ref[...], w_ref[...], preferred_element_type=jnp.float32)
    o_ref[...] = x_ref[...] + _rms(y, g_ref[...])


def matmul_norm_residual(a_bf16, w_stack, layer, g, x2d):
    m, k = a_bf16.shape
    d = w_stack.shape[2]
    return pl.pallas_call(
        _matmul_norm_res_kernel,
        grid=(m // ROW_TILE,),
        in_specs=[pl.BlockSpec((ROW_TILE, k), lambda i: (i, 0)),
                  _layer_spec(w_stack.shape, layer),
                  _const_spec((1, d)),
                  pl.BlockSpec((ROW_TILE, d), lambda i: (i, 0))],
        out_specs=pl.BlockSpec((ROW_TILE, d), lambda i: (i, 0)),
        out_shape=jax.ShapeDtypeStruct((m, d), jnp.float32),
        compiler_params=_params(("parallel",)),
        name="matmul_norm_residual",
    )(a_bf16, w_stack, g.reshape(1, d), x2d)


def _ffn_up_kernel(ahalo_ref, a_ref, wo_ref, gpost_ref, halo_ref, x_ref, g_ref,
                   wup_ref, cw_ref, cb_ref, x1_ref, o_ref,
                   h_scr, ug_scr, uv_scr, *, seq_len):
    i = pl.program_id(0)
    tm = x_ref.shape[0]
    a_ext = jnp.concatenate([ahalo_ref[...], a_ref[...]], axis=0)
    y = jnp.dot(a_ext, wo_ref[...], preferred_element_type=jnp.float32)
    x_ext = jnp.concatenate([halo_ref[...], x_ref[...]], axis=0)
    x1 = x_ext + _rms(y, gpost_ref[...])
    x1_ref[...] = x1[HALO:, :]
    g = g_ref[...]
    first = (i * tm) % seq_len == 0
    h_scr[0:HALO, :] = _rms(jnp.where(first, 0.0, x1[0:HALO, :]),
                            g).astype(jnp.bfloat16)
    h_scr[HALO:, :] = _rms(x1[HALO:, :], g).astype(jnp.bfloat16)

    def conv(u_scr, slab, col):
        c = cb_ref[:, col:col + LANES]
        for tap in range(CONV_WIDTH):
            off = HALO - (CONV_WIDTH - 1) + tap
            rows = pl.ds(off, tm) if off % 8 == 0 else pl.ds(off, tm, stride=1)
            c = c + cw_ref[tap:tap + 1, col:col + LANES] * u_scr[slab, rows, :]
        return c

    slabs = FF_CHUNK // LANES
    for j in range(D_FF // FF_CHUNK):
        cg = j * FF_CHUNK
        cv = D_FF + cg
        ug = jnp.dot(h_scr[...], wup_ref[:, cg:cg + FF_CHUNK],
                     preferred_element_type=jnp.float32)
        uv = jnp.dot(h_scr[...], wup_ref[:, cv:cv + FF_CHUNK],
                     preferred_element_type=jnp.float32)
        base = (j % 2) * slabs
        for s in range(slabs):
            ug_scr[base + s] = ug[:, s * LANES:(s + 1) * LANES]
            uv_scr[base + s] = uv[:, s * LANES:(s + 1) * LANES]
        for s in range(slabs):
            gate = conv(ug_scr, base + s, cg + s * LANES)
            val = conv(uv_scr, base + s, cv + s * LANES)
            t = jnp.tanh(gate * (GELU_C0 + GELU_C1 * (gate * gate)))
            o_ref[:, cg + s * LANES:cg + (s + 1) * LANES] = (
                (gate * val) * (0.5 * t + 0.5)).astype(o_ref.dtype)


def mixer_out_ffn_up(a_bf16, wo_stack, wo_layer, g_post, x2d, g, wup_stack,
                     conv_w_stack, layer, conv_b, seq_len):
    m, d = x2d.shape
    ka = a_bf16.shape[1]
    n2 = wup_stack.shape[2]
    tm = FFN_ROW_TILE
    halo_blocks = tm // HALO
    u_scratch = pltpu.VMEM((2 * FF_CHUNK // LANES, tm + HALO, LANES), jnp.float32)

    def halo_map(i):
        return (jnp.maximum(i * halo_blocks - 1, 0), 0)

    def tile_map(i):
        return (i, 0)

    return pl.pallas_call(
        functools.partial(_ffn_up_kernel, seq_len=seq_len),
        grid=(m // tm,),
        in_specs=[pl.BlockSpec((HALO, ka), halo_map),
                  pl.BlockSpec((tm, ka), tile_map),
                  _layer_spec(wo_stack.shape, wo_layer),
                  _const_spec((1, d)),
                  pl.BlockSpec((HALO, d), halo_map),
                  pl.BlockSpec((tm, d), tile_map),
                  _const_spec((1, d)),
                  _layer_spec(wup_stack.shape, layer),
                  _layer_spec(conv_w_stack.shape, layer),
                  _const_spec((1, n2))],
        out_specs=[pl.BlockSpec((tm, d), tile_map),
                   pl.BlockSpec((tm, D_FF), tile_map)],
        out_shape=[jax.ShapeDtypeStruct((m, d), jnp.float32),
                   jax.ShapeDtypeStruct((m, D_FF), jnp.bfloat16)],
        scratch_shapes=[pltpu.VMEM((tm + HALO, d), jnp.bfloat16),
                        u_scratch, u_scratch],
        compiler_params=_params(("parallel",)),
        name="mixer_out_ffn_up",
    )(a_bf16, a_bf16, wo_stack, g_post.reshape(1, d), x2d, x2d, g.reshape(1, d),
      wup_stack, conv_w_stack, conv_b.reshape(1, n2))


def _swa_kernel(slopes_ref, sinks_ref, q_ref, k_ref, kp_ref, v_ref, vp_ref,
                o_ref, qt_all, k_aug, vt_aug, p_scr):
    t = pl.program_id(1)
    f32, bf16 = jnp.float32, jnp.bfloat16
    d, grp = SWA_HEAD_DIM, SWA_GROUP
    ncol = SWA_HEADS * BLOCK
    kvw = SWA_KV_HEADS * d

    @pl.when((pl.program_id(0) == 0) & (t == 0))
    def _init_constants():
        colh = lax.broadcasted_iota(jnp.int32, (16, ncol), 1) >> 7
        r = lax.broadcasted_iota(jnp.int32, (16, ncol), 0)
        a = jnp.zeros((16, ncol), f32)
        for h in range(SWA_HEADS):
            a = jnp.where(colh == h, slopes_ref[h] * LOG2E, a)
        a0 = a.astype(bf16).astype(f32)
        a1 = (a - a0).astype(bf16).astype(f32)
        a2 = a - a0 - a1
        coef = jnp.where(r == 0, a0, jnp.where(r == 1, a1,
                                               jnp.where(r == 2, a2, 0.0)))
        kj = lax.broadcasted_iota(jnp.int32, (2 * BLOCK, kvw), 0).astype(f32)
        lane = lax.broadcasted_iota(jnp.int32, (2 * BLOCK, kvw), 1)
        for par in range(SWA_TILE // BLOCK):
            qt_all[par] = jnp.zeros((2 * kvw, ncol), bf16)
            qt_all[par, kvw:kvw + 16, :] = coef.astype(bf16)
            k_aug[par, :, kvw:2 * kvw] = jnp.where(lane < 3, kj, 0.0).astype(bf16)
            for kv in range(SWA_KV_HEADS):
                vt_aug[par, kv, d:SWA_VROWS, :] = jnp.ones(
                    (SWA_VROWS - d, 2 * BLOCK), bf16)

    key = lax.broadcasted_iota(jnp.int32, (2 * BLOCK, BLOCK), 0)
    qry = lax.broadcasted_iota(jnp.int32, (2 * BLOCK, BLOCK), 1)
    dist = qry + BLOCK - key
    band = (dist >= 0) & (dist < BLOCK)
    band_first = band & (key >= jnp.where(t > 0, 0, BLOCK))
    qpos = (lax.broadcasted_iota(jnp.int32, (1, BLOCK), 1) + BLOCK).astype(f32)

    for blk in range(SWA_TILE // BLOCK):
        par = blk
        r0 = blk * BLOCK
        valid = band_first if blk == 0 else band
        k_prev = kp_ref[...] if blk == 0 else k_ref[r0 - BLOCK:r0, :]
        v_prev = vp_ref[...] if blk == 0 else v_ref[r0 - BLOCK:r0, :]
        k_aug[par, 0:BLOCK, 0:kvw] = k_prev
        k_aug[par, BLOCK:2 * BLOCK, 0:kvw] = k_ref[r0:r0 + BLOCK, :]
        v_t = jnp.concatenate([v_prev, v_ref[r0:r0 + BLOCK, :]],
                              axis=0).astype(f32).T
        for kv in range(SWA_KV_HEADS):
            vt_aug[par, kv, 0:d, :] = v_t[kv * d:(kv + 1) * d, :].astype(bf16)
        q_t = (q_ref[r0:r0 + BLOCK, :].astype(f32) * (LOG2E * d ** -0.5)).T
        for h in range(SWA_HEADS):
            kv = h // grp
            qt_all[par, kv * d:(kv + 1) * d, h * BLOCK:(h + 1) * BLOCK] = (
                q_t[h * d:(h + 1) * d, :].astype(bf16))

        s = jnp.dot(k_aug[par], qt_all[par], preferred_element_type=f32)
        m_list, sink_list = [], []
        for h in range(SWA_HEADS):
            s_h = jnp.where(valid, s[:, h * BLOCK:(h + 1) * BLOCK], NEG_BIG)
            sink_h = (sinks_ref[h] * LOG2E
                      + (slopes_ref[h] * LOG2E) * qpos)
            m_h = jnp.maximum(jnp.max(s_h, axis=0, keepdims=True), sink_h)
            p_scr[par, :, h * BLOCK:(h + 1) * BLOCK] = (
                jnp.exp2(s_h - m_h).astype(bf16))
            m_list.append(m_h)
            sink_list.append(sink_h)

        for kv in range(SWA_KV_HEADS):
            pv = jnp.dot(vt_aug[par, kv],
                         p_scr[par, :, kv * grp * BLOCK:(kv + 1) * grp * BLOCK],
                         preferred_element_type=f32)
            for pair in range(grp // 2):
                halves = []
                for g in (2 * pair, 2 * pair + 1):
                    h = kv * grp + g
                    cols = slice(g * BLOCK, (g + 1) * BLOCK)
                    denom = pv[d:d + 1, cols] + jnp.exp2(
                        sink_list[h] - m_list[h])
                    halves.append(pv[0:d, cols] / denom)
                o_pair = jnp.concatenate(halves, axis=0).T
                c0 = (kv * grp + 2 * pair) * d
                o_ref[r0:r0 + BLOCK, c0:c0 + 2 * d] = o_pair.astype(o_ref.dtype)


def swa_attention(qkv, slopes, sinks, batch, seq_len):
    m = qkv.shape[0]
    nt = seq_len // SWA_TILE
    per_tile = SWA_TILE // BLOCK
    hq = SWA_HEADS * SWA_HEAD_DIM
    kv_w = SWA_KV_HEADS * SWA_HEAD_DIM
    k_col = hq // kv_w
    v_col = k_col + 1
    ncol = SWA_HEADS * BLOCK

    def cur(col):
        return lambda b, t: (b * nt + t, col)

    def prev(col):
        return lambda b, t: (jnp.maximum((b * nt + t) * per_tile - 1, 0), col)

    smem = pl.BlockSpec(memory_space=pltpu.SMEM)
    return pl.pallas_call(
        _swa_kernel,
        grid=(batch, nt),
        in_specs=[smem, smem,
                  pl.BlockSpec((SWA_TILE, hq), cur(0)),
                  pl.BlockSpec((SWA_TILE, kv_w), cur(k_col)),
                  pl.BlockSpec((BLOCK, kv_w), prev(k_col)),
                  pl.BlockSpec((SWA_TILE, kv_w), cur(v_col)),
                  pl.BlockSpec((BLOCK, kv_w), prev(v_col))],
        out_specs=pl.BlockSpec((SWA_TILE, hq), cur(0)),
        out_shape=jax.ShapeDtypeStruct((m, hq), jnp.bfloat16),
        scratch_shapes=[pltpu.VMEM((per_tile, 2 * kv_w, ncol), jnp.bfloat16),
                        pltpu.VMEM((per_tile, 2 * BLOCK, 2 * kv_w), jnp.bfloat16),
                        pltpu.VMEM((per_tile, SWA_KV_HEADS, SWA_VROWS, 2 * BLOCK),
                                   jnp.bfloat16),
                        pltpu.VMEM((per_tile, 2 * BLOCK, ncol), jnp.bfloat16)],
        compiler_params=_params(("arbitrary", "arbitrary")),
        name="swa_attention",
    )(slopes, sinks, qkv, qkv, qkv, qkv, qkv)


def _diff_kernel(slopes_ref, lamv_ref, subg_ref, q_ref, k_ref, v_ref, o_ref,
                 kaug, vt, qaug_t, s_tiles, s_pre, *, lambda_init, seq_len):
    hp = pl.program_id(1)
    tq, tk, d = DIFF_TQ, DIFF_TK, DIFF_HEAD_DIM
    hw = 2 * d
    nq = seq_len // tq
    f32, bf16 = jnp.float32, jnp.bfloat16
    heads = range(DIFF_HEADS_PER_STEP)

    def stage_keys_values():
        lane = lax.broadcasted_iota(jnp.int32, (tk, hw), 1)
        row = lax.broadcasted_iota(jnp.int32, (tk, hw), 0)
        for c in range(seq_len // tk):
            r0 = c * tk
            pos = row + r0
            hi = (pos >> 6).astype(f32)
            lo = (pos & 63).astype(f32)
            aux = jnp.where(lane < 3, hi,
                            jnp.where(lane < 6, lo,
                                      jnp.where(lane < 9, 1.0, 0.0)))
            for g in heads:
                cols = slice(g * hw, (g + 1) * hw)
                kaug[g, c, :, 0:hw] = k_ref[r0:r0 + tk, cols]
                kaug[g, c, :, hw:2 * hw] = aux.astype(bf16)
                vt[g, c, 0:hw, :] = (
                    v_ref[r0:r0 + tk, cols].astype(f32).T.astype(bf16))
                vt[g, c, hw:DIFF_VROWS, :] = jnp.ones((DIFF_VROWS - hw, tk), bf16)

    def setup_queries(g, tile):
        q0 = tile * tq
        q_t = (q_ref[q0:q0 + tq, g * hw:(g + 1) * hw].astype(f32)
               * (LOG2E * d ** -0.5)).T
        qrow = lax.broadcasted_iota(jnp.int32, (hw, tq), 0)
        par = tile % 2
        qaug_t[g, par, 0:hw, 0:tq] = jnp.where(qrow < d, q_t, 0.0).astype(bf16)
        qaug_t[g, par, 0:hw, tq:2 * tq] = jnp.where(qrow >= d, q_t,
                                                    0.0).astype(bf16)
        a_coef = slopes_ref[hp * DIFF_HEADS_PER_STEP + g] * LOG2E
        c_coef = -a_coef * float(q0)
        r = lax.broadcasted_iota(jnp.int32, (16, 2 * tq), 0)
        t = jnp.where(r < 3, a_coef * 64.0,
                      jnp.where(r < 6, a_coef, jnp.where(r < 9, c_coef, 0.0)))
        t0 = t.astype(bf16).astype(f32)
        t1 = (t - t0).astype(bf16).astype(f32)
        t2 = t - t0 - t1
        first = (r == 0) | (r == 3) | (r == 6)
        second = (r == 1) | (r == 4) | (r == 7)
        qaug_t[g, par, hw:hw + 16, :] = jnp.where(
            first, t0, jnp.where(second, t1, t2)).astype(bf16)
        qaug_t[g, par, hw + 16:2 * hw, :] = jnp.zeros((hw - 16, 2 * tq), bf16)

    half = tk // 2
    late_cols = (slice(half, tq), slice(tq + half, 2 * tq))

    def scores(tile, j, buf, slot, diagonal=False):
        par = tile % 2
        for g in heads:
            if diagonal:
                buf[g, slot, 0:half, :] = jnp.dot(
                    kaug[g, j, 0:half, :], qaug_t[g, par],
                    preferred_element_type=f32)
                for cols in late_cols:
                    buf[g, slot, half:tk, cols] = jnp.dot(
                        kaug[g, j, half:tk, :], qaug_t[g, par, :, cols],
                        preferred_element_type=f32)
            else:
                s = jnp.dot(kaug[g, j], qaug_t[g, par],
                            preferred_element_type=f32)
                buf[g, slot, 0:tk, :] = s
                buf[g, slot, tk:tk + 1, :] = jnp.max(s, axis=0, keepdims=True)

    def accumulate(m_old, acc, m_new, pv):
        if m_old is None:
            return pv
        return acc * jnp.exp2(m_old - m_new) + pv

    def softmax_pv(j, buf, slot, m_old, acc):
        m_out, acc_out = [], []
        for g in heads:
            s = buf[g, slot, 0:tk, :]
            col_max = buf[g, slot, tk:tk + 1, :]
            m_new = (col_max if m_old[g] is None
                     else jnp.maximum(m_old[g], col_max))
            p = jnp.exp2(s - m_new).astype(bf16)
            pv = jnp.dot(vt[g, j], p, preferred_element_type=f32)
            acc_out.append(accumulate(m_old[g], acc[g], m_new, pv))
            m_out.append(m_new)
        return m_out, acc_out

    def softmax_pv_diagonal(j, buf, slot, m_old, acc):
        m_out, acc_out = [], []
        row = lax.broadcasted_iota(jnp.int32, (half, half), 0)
        col = lax.broadcasted_iota(jnp.int32, (half, half), 1)
        tri = row <= col
        for g in heads:
            top, bot, col_max = [], [], []
            for b in range(2 * tq // half):
                cols = slice(b * half, (b + 1) * half)
                s_top = buf[g, slot, 0:half, cols]
                if b % 2 == 0:
                    s_top = jnp.where(tri, s_top, NEG_BIG)
                    s_bot = None
                    cm = jnp.max(s_top, axis=0, keepdims=True)
                else:
                    s_bot = jnp.where(tri, buf[g, slot, half:tk, cols], NEG_BIG)
                    cm = jnp.maximum(jnp.max(s_top, axis=0, keepdims=True),
                                     jnp.max(s_bot, axis=0, keepdims=True))
                top.append(s_top)
                bot.append(s_bot)
                col_max.append(cm)
            col_max = jnp.concatenate(col_max, axis=1)
            m_new = (col_max if m_old[g] is None
                     else jnp.maximum(m_old[g], col_max))
            p_top = jnp.concatenate(
                [jnp.exp2(s_b - m_new[:, b * half:(b + 1) * half]).astype(bf16)
                 for b, s_b in enumerate(top)], axis=1)
            pv = jnp.dot(vt[g, j, :, 0:half], p_top, preferred_element_type=f32)
            parts = []
            for b, s_b in enumerate(bot):
                cols = slice(b * half, (b + 1) * half)
                if s_b is None:
                    parts.append(pv[:, cols])
                else:
                    p_b = jnp.exp2(s_b - m_new[:, cols]).astype(bf16)
                    parts.append(pv[:, cols] + jnp.dot(
                        vt[g, j, :, half:tk], p_b, preferred_element_type=f32))
            pv = jnp.concatenate(parts, axis=1)
            acc_out.append(accumulate(m_old[g], acc[g], m_new, pv))
            m_out.append(m_new)
        return m_out, acc_out

    def normalise_and_store(qi, acc_all):
        lam = (jnp.exp(jnp.sum(lamv_ref[0:1, :] * lamv_ref[1:2, :]))
               - jnp.exp(jnp.sum(lamv_ref[2:3, :] * lamv_ref[3:4, :]))
               + lambda_init)
        for g in heads:
            acc = acc_all[g]
            o_all = acc[0:hw, :] * (1.0 / acc[hw:hw + 1, :])
            o_t = o_all[:, 0:tq] - lam * o_all[:, tq:2 * tq]
            o_t = o_t * lax.rsqrt(
                jnp.mean(o_t * o_t, axis=0, keepdims=True) + EPS)
            o = o_t.T * (subg_ref[...] * (1.0 - lambda_init))
            o_ref[qi * tq:(qi + 1) * tq,
                  g * hw:(g + 1) * hw] = o.astype(o_ref.dtype)

    def query_tile(c):
        base = (c % 2) * (nq - 1)
        m = [None] * len(heads)
        acc = [None] * len(heads)
        for j in range(c + 1):
            if j < c:
                scores(c, j + 1, s_tiles, base + j, diagonal=(j + 1 == c))
            buf, slot = (s_pre, c % 2) if j == 0 else (s_tiles, base + j - 1)
            step = softmax_pv_diagonal if j == c else softmax_pv
            m, acc = step(j, buf, slot, m, acc)
        if c + 1 < nq:
            for g in heads:
                setup_queries(g, c + 1)
            scores(c + 1, 0, s_pre, (c + 1) % 2)
        normalise_and_store(c, acc)

    stage_keys_values()
    for g in heads:
        setup_queries(g, 0)
    scores(0, 0, s_pre, 0, diagonal=True)

    def pair_of_tiles(c):
        query_tile(c)
        query_tile(c + 1)

    def tile_step(i, carry):
        for c in range(0, nq, 2):
            pl.when(i == c // 2)(functools.partial(pair_of_tiles, c))
        return carry

    lax.fori_loop(0, nq // 2, tile_step, 0)


def diff_attention(qkv, slopes, lam_vecs, subln_g, lambda_init, batch, seq_len):
    m = qkv.shape[0]
    hw = 2 * DIFF_HEAD_DIM
    nq = seq_len // DIFF_TQ
    smem = pl.BlockSpec(memory_space=pltpu.SMEM)
    assert DIFF_TQ == DIFF_TK and seq_len % DIFF_TK == 0
    nk = seq_len // DIFF_TK
    hps = DIFF_HEADS_PER_STEP
    groups = DIFF_HEADS // hps
    bw = hps * hw
    score_tile = (DIFF_TK + 8, 2 * DIFF_TQ)
    return pl.pallas_call(
        functools.partial(_diff_kernel, lambda_init=lambda_init,
                          seq_len=seq_len),
        grid=(batch, groups),
        in_specs=[smem,
                  _const_spec(lam_vecs.shape),
                  _const_spec((1, hw)),
                  pl.BlockSpec((seq_len, bw), lambda b, h: (b, h)),
                  pl.BlockSpec((seq_len, bw), lambda b, h: (b, groups + h)),
                  pl.BlockSpec((seq_len, bw), lambda b, h: (b, 2 * groups + h))],
        out_specs=pl.BlockSpec((seq_len, bw), lambda b, h: (b, h)),
        out_shape=jax.ShapeDtypeStruct((m, DIFF_HEADS * hw), jnp.bfloat16),
        scratch_shapes=[pltpu.VMEM((hps, nk, DIFF_TK, 2 * hw), jnp.bfloat16),
                        pltpu.VMEM((hps, nk, DIFF_VROWS, DIFF_TK), jnp.bfloat16),
                        pltpu.VMEM((hps, 2, 2 * hw, 2 * DIFF_TQ), jnp.bfloat16),
                        pltpu.VMEM((hps, 2 * (nk - 1)) + score_tile, jnp.float32),
                        pltpu.VMEM((hps, 2) + score_tile, jnp.float32)],
        compiler_params=_params(("parallel", "parallel")),
        name="diff_attention",
    )(slopes, lam_vecs, subln_g.reshape(1, hw), qkv, qkv, qkv)


def _alibi_slopes(n_heads):
    return np.exp2(-8.0 * (np.arange(n_heads, dtype=np.float32) + 1.0)
                   / n_heads).astype(np.float32)


def _diff_lambda_init(layer_idx):
    return 0.8 - 0.6 * math.exp(-0.3 * layer_idx)


def kernel(x, mix_pre_g, mix_post_g, ffn_pre_g, ffn_post_g, swa_w_qkv, swa_sinks, swa_w_o, diff_w_qkv, diff_lam_q1, diff_lam_k1, diff_lam_q2, diff_lam_k2, diff_subln_g, diff_w_o, ffn_w_up, ffn_conv_w, ffn_conv_b, ffn_w_down):
    batch, seq_len, d = x.shape
    depth = mix_pre_g.shape[0]
    bf = jnp.bfloat16
    swa_w_qkv, swa_w_o = swa_w_qkv.astype(bf), swa_w_o.astype(bf)
    diff_w_qkv, diff_w_o = diff_w_qkv.astype(bf), diff_w_o.astype(bf)
    ffn_w_up, ffn_w_down = ffn_w_up.astype(bf), ffn_w_down.astype(bf)
    x2d = x.reshape(batch * seq_len, d)
    for i in range(depth):
        j = i // N_MIXERS
        if i % N_MIXERS == 0:
            qkv = norm_matmul(x2d, mix_pre_g[i], swa_w_qkv, j, n_chunk=640)
            a = swa_attention(qkv, _alibi_slopes(SWA_HEADS),
                              swa_sinks[j].astype(jnp.float32), batch, seq_len)
            w_o = swa_w_o
        else:
            qkv = norm_matmul(x2d, mix_pre_g[i], diff_w_qkv, j)
            lam_vecs = jnp.stack([diff_lam_q1[j], diff_lam_k1[j],
                                  diff_lam_q2[j], diff_lam_k2[j]]).astype(jnp.float32)
            a = diff_attention(qkv, _alibi_slopes(DIFF_HEADS), lam_vecs,
                               diff_subln_g[j], _diff_lambda_init(i),
                               batch, seq_len)
            w_o = diff_w_o
        x2d, act = mixer_out_ffn_up(a, w_o, j, mix_post_g[i], x2d, ffn_pre_g[i],
                                    ffn_w_up, ffn_conv_w, i, ffn_conv_b[i],
                                    seq_len)
        x2d = matmul_norm_residual(act, ffn_w_down, i, ffn_post_g[i], x2d)
    return x2d.reshape(batch, seq_len, d)
```

```python
import functools
import math

import jax
import jax.numpy as jnp
import numpy as np
from jax import lax
from jax.experimental import pallas as pl
from jax.experimental.pallas import tpu as pltpu

D_MODEL = 1024
EPS = 1e-6
BLOCK = 128
SWA_HEADS = 16
SWA_KV_HEADS = 2
SWA_HEAD_DIM = 64
SWA_GROUP = SWA_HEADS // SWA_KV_HEADS
DIFF_HEADS = 8
DIFF_HEAD_DIM = 64
D_FF = 2816
CONV_WIDTH = 3
N_MIXERS = 2

VMEM_LIMIT_BYTES = 56 * 1024 * 1024
LANES = 128
HALO = 16
GELU_C0 = math.sqrt(2.0 / math.pi)
GELU_C1 = 0.044715 * GELU_C0
NEG_BIG = -1e30

ROW_TILE = 1024
FFN_ROW_TILE = 1024
FF_CHUNK = 256
SWA_TILE = 1024
SWA_VROWS = 80
DIFF_TQ = 512
DIFF_TK = 512
DIFF_VROWS = 144
DIFF_HEADS_PER_STEP = 1
LOG2E = 1.4426950408889634


def _rms(x, g):
    return x * lax.rsqrt(jnp.mean(x * x, axis=-1, keepdims=True) + EPS) * g


def _params(sem):
    return pltpu.CompilerParams(dimension_semantics=sem,
                                vmem_limit_bytes=VMEM_LIMIT_BYTES)


def _layer_spec(shape, layer):
    nd = len(shape)
    return pl.BlockSpec((None,) + tuple(shape[1:]),
                        lambda *_: (layer,) + (0,) * (nd - 1),
                        pipeline_mode=pl.Buffered(1))


def _const_spec(shape):
    nd = len(shape)
    return pl.BlockSpec(shape, lambda *_: (0,) * nd,
                        pipeline_mode=pl.Buffered(1))


def _norm_matmul_kernel(x_ref, g_ref, w_ref, o_ref, *, n_chunk):
    h = _rms(x_ref[...], g_ref[...]).astype(jnp.bfloat16)
    n = o_ref.shape[1]
    for c in range(0, n, n_chunk):
        o_ref[:, c:c + n_chunk] = jnp.dot(
            h, w_ref[:, c:c + n_chunk],
            preferred_element_type=jnp.float32).astype(o_ref.dtype)


def norm_matmul(x2d, g, w_stack, layer, n_chunk=512):
    m, d = x2d.shape
    n = w_stack.shape[2]
    return pl.pallas_call(
        functools.partial(_norm_matmul_kernel, n_chunk=n_chunk),
        grid=(m // ROW_TILE,),
        in_specs=[pl.BlockSpec((ROW_TILE, d), lambda i: (i, 0)),
                  _const_spec((1, d)),
                  _layer_spec(w_stack.shape, layer)],
        out_specs=pl.BlockSpec((ROW_TILE, n), lambda i: (i, 0)),
        out_shape=jax.ShapeDtypeStruct((m, n), jnp.bfloat16),
        compiler_params=_params(("parallel",)),
        name="norm_matmul",
    )(x2d, g.reshape(1, d), w_stack)


def _matmul_norm_res_kernel(a_ref, w_ref, g_ref, x_ref, o_ref):
    y = jnp.dot(a_ref[...], w_ref[...], preferred_element_type=jnp.float32)
    o_ref[...] = x_ref[...] + _rms(y, g_ref[...])


def matmul_norm_residual(a_bf16, w_stack, layer, g, x2d):
    m, k = a_bf16.shape
    d = w_stack.shape[2]
    return pl.pallas_call(
        _matmul_norm_res_kernel,
        grid=(m // ROW_TILE,),
        in_specs=[pl.BlockSpec((ROW_TILE, k), lambda i: (i, 0)),
                  _layer_spec(w_stack.shape, layer),
                  _const_spec((1, d)),
                  pl.BlockSpec((ROW_TILE, d), lambda i: (i, 0))],
        out_specs=pl.BlockSpec((ROW_TILE, d), lambda i: (i, 0)),
        out_shape=jax.ShapeDtypeStruct((m, d), jnp.float32),
        compiler_params=_params(("parallel",)),
        name="matmul_norm_residual",
    )(a_bf16, w_stack, g.reshape(1, d), x2d)


def _ffn_up_kernel(ahalo_ref, a_ref, wo_ref, gpost_ref, halo_ref, x_ref, g_ref,
                   wup_ref, cw_ref, cb_ref, x1_ref, o_ref,
                   h_scr, ug_scr, uv_scr, *, seq_len):
    i = pl.program_id(0)
    tm = x_ref.shape[0]
    a_ext = jnp.concatenate([ahalo_ref[...], a_ref[...]], axis=0)
    y = jnp.dot(a_ext, wo_ref[...], preferred_element_type=jnp.float32)
    x_ext = jnp.concatenate([halo_ref[...], x_ref[...]], axis=0)
    x1 = x_ext + _rms(y, gpost_ref[...])
    x1_ref[...] = x1[HALO:, :]
    g = g_ref[...]
    first = (i * tm) % seq_len == 0
    h_scr[0:HALO, :] = _rms(jnp.where(first, 0.0, x1[0:HALO, :]),
                            g).astype(jnp.bfloat16)
    h_scr[HALO:, :] = _rms(x1[HALO:, :], g).astype(jnp.bfloat16)

    def conv(u_scr, slab, col):
        c = cb_ref[:, col:col + LANES]
        for tap in range(CONV_WIDTH):
            off = HALO - (CONV_WIDTH - 1) + tap
            rows = pl.ds(off, tm) if off % 8 == 0 else pl.ds(off, tm, stride=1)
            c = c + cw_ref[tap:tap + 1, col:col + LANES] * u_scr[slab, rows, :]
        return c

    slabs = FF_CHUNK // LANES
    for j in range(D_FF // FF_CHUNK):
        cg = j * FF_CHUNK
        cv = D_FF + cg
        ug = jnp.dot(h_scr[...], wup_ref[:, cg:cg + FF_CHUNK],
                     preferred_element_type=jnp.float32)
        uv = jnp.dot(h_scr[...], wup_ref[:, cv:cv + FF_CHUNK],
                     preferred_element_type=jnp.float32)
        base = (j % 2) * slabs
        for s in range(slabs):
            ug_scr[base + s] = ug[:, s * LANES:(s + 1) * LANES]
            uv_scr[base + s] = uv[:, s * LANES:(s + 1) * LANES]
        for s in range(slabs):
            gate = conv(ug_scr, base + s, cg + s * LANES)
            val = conv(uv_scr, base + s, cv + s * LANES)
            t = jnp.tanh(gate * (GELU_C0 + GELU_C1 * (gate * gate)))
            o_ref[:, cg + s * LANES:cg + (s + 1) * LANES] = (
                (gate * val) * (0.5 * t + 0.5)).astype(o_ref.dtype)


def mixer_out_ffn_up(a_bf16, wo_stack, wo_layer, g_post, x2d, g, wup_stack,
                     conv_w_stack, layer, conv_b, seq_len):
    m, d = x2d.shape
    ka = a_bf16.shape[1]
    n2 = wup_stack.shape[2]
    tm = FFN_ROW_TILE
    halo_blocks = tm // HALO
    u_scratch = pltpu.VMEM((2 * FF_CHUNK // LANES, tm + HALO, LANES), jnp.float32)

    def halo_map(i):
        return (jnp.maximum(i * halo_blocks - 1, 0), 0)

    def tile_map(i):
        return (i, 0)

    return pl.pallas_call(
        functools.partial(_ffn_up_kernel, seq_len=seq_len),
        grid=(m // tm,),
        in_specs=[pl.BlockSpec((HALO, ka), halo_map),
                  pl.BlockSpec((tm, ka), tile_map),
                  _layer_spec(wo_stack.shape, wo_layer),
                  _const_spec((1, d)),
                  pl.BlockSpec((HALO, d), halo_map),
                  pl.BlockSpec((tm, d), tile_map),
                  _const_spec((1, d)),
                  _layer_spec(wup_stack.shape, layer),
                  _layer_spec(conv_w_stack.shape, layer),
                  _const_spec((1, n2))],
        out_specs=[pl.BlockSpec((tm, d), tile_map),
                   pl.BlockSpec((tm, D_FF), tile_map)],
        out_shape=[jax.ShapeDtypeStruct((m, d), jnp.float32),
                   jax.ShapeDtypeStruct((m, D_FF), jnp.bfloat16)],
        scratch_shapes=[pltpu.VMEM((tm + HALO, d), jnp.bfloat16),
                        u_scratch, u_scratch],
        compiler_params=_params(("parallel",)),
        name="mixer_out_ffn_up",
    )(a_bf16, a_bf16, wo_stack, g_post.reshape(1, d), x2d, x2d, g.reshape(1, d),
      wup_stack, conv_w_stack, conv_b.reshape(1, n2))


def _swa_kernel(slopes_ref, sinks_ref, q_ref, k_ref, kp_ref, v_ref, vp_ref,
                o_ref, qt_all, k_aug, vt_aug, p_scr):
    t = pl.program_id(1)
    f32, bf16 = jnp.float32, jnp.bfloat16
    d, grp = SWA_HEAD_DIM, SWA_GROUP
    ncol = SWA_HEADS * BLOCK
    kvw = SWA_KV_HEADS * d

    @pl.when((pl.program_id(0) == 0) & (t == 0))
    def _init_constants():
        colh = lax.broadcasted_iota(jnp.int32, (16, ncol), 1) >> 7
        r = lax.broadcasted_iota(jnp.int32, (16, ncol), 0)
        a = jnp.zeros((16, ncol), f32)
        for h in range(SWA_HEADS):
            a = jnp.where(colh == h, slopes_ref[h] * LOG2E, a)
        a0 = a.astype(bf16).astype(f32)
        a1 = (a - a0).astype(bf16).astype(f32)
        a2 = a - a0 - a1
        coef = jnp.where(r == 0, a0, jnp.where(r == 1, a1,
                                               jnp.where(r == 2, a2, 0.0)))
        kj = lax.broadcasted_iota(jnp.int32, (2 * BLOCK, kvw), 0).astype(f32)
        lane = lax.broadcasted_iota(jnp.int32, (2 * BLOCK, kvw), 1)
        for par in range(SWA_TILE // BLOCK):
            qt_all[par] = jnp.zeros((2 * kvw, ncol), bf16)
            qt_all[par, kvw:kvw + 16, :] = coef.astype(bf16)
            k_aug[par, :, kvw:2 * kvw] = jnp.where(lane < 3, kj, 0.0).astype(bf16)
            for kv in range(SWA_KV_HEADS):
                vt_aug[par, kv, d:SWA_VROWS, :] = jnp.ones(
                    (SWA_VROWS - d, 2 * BLOCK), bf16)

    key = lax.broadcasted_iota(jnp.int32, (2 * BLOCK, BLOCK), 0)
    qry = lax.broadcasted_iota(jnp.int32, (2 * BLOCK, BLOCK), 1)
    dist = qry + BLOCK - key
    band = (dist >= 0) & (dist < BLOCK)
    band_first = band & (key >= jnp.where(t > 0, 0, BLOCK))
    qpos = (lax.broadcasted_iota(jnp.int32, (1, BLOCK), 1) + BLOCK).astype(f32)

    for blk in range(SWA_TILE // BLOCK):
        par = blk
        r0 = blk * BLOCK
        valid = band_first if blk == 0 else band
        k_prev = kp_ref[...] if blk == 0 else k_ref[r0 - BLOCK:r0, :]
        v_prev = vp_ref[...] if blk == 0 else v_ref[r0 - BLOCK:r0, :]
        k_aug[par, 0:BLOCK, 0:kvw] = k_prev
        k_aug[par, BLOCK:2 * BLOCK, 0:kvw] = k_ref[r0:r0 + BLOCK, :]
        v_t = jnp.concatenate([v_prev, v_ref[r0:r0 + BLOCK, :]],
                              axis=0).astype(f32).T
        for kv in range(SWA_KV_HEADS):
            vt_aug[par, kv, 0:d, :] = v_t[kv * d:(kv + 1) * d, :].astype(bf16)
        q_t = (q_ref[r0:r0 + BLOCK, :].astype(f32) * (LOG2E * d ** -0.5)).T
        for h in range(SWA_HEADS):
            kv = h // grp
            qt_all[par, kv * d:(kv + 1) * d, h * BLOCK:(h + 1) * BLOCK] = (
                q_t[h * d:(h + 1) * d, :].astype(bf16))

        s = jnp.dot(k_aug[par], qt_all[par], preferred_element_type=f32)
        m_list, sink_list = [], []
        for h in range(SWA_HEADS):
            s_h = jnp.where(valid, s[:, h * BLOCK:(h + 1) * BLOCK], NEG_BIG)
            sink_h = (sinks_ref[h] * LOG2E
                      + (slopes_ref[h] * LOG2E) * qpos)
            m_h = jnp.maximum(jnp.max(s_h, axis=0, keepdims=True), sink_h)
            p_scr[par, :, h * BLOCK:(h + 1) * BLOCK] = (
                jnp.exp2(s_h - m_h).astype(bf16))
            m_list.append(m_h)
            sink_list.append(sink_h)

        for kv in range(SWA_KV_HEADS):
            pv = jnp.dot(vt_aug[par, kv],
                         p_scr[par, :, kv * grp * BLOCK:(kv + 1) * grp * BLOCK],
                         preferred_element_type=f32)
            for pair in range(grp // 2):
                halves = []
                for g in (2 * pair, 2 * pair + 1):
                    h = kv * grp + g
                    cols = slice(g * BLOCK, (g + 1) * BLOCK)
                    denom = pv[d:d + 1, cols] + jnp.exp2(
                        sink_list[h] - m_list[h])
                    halves.append(pv[0:d, cols] / denom)
                o_pair = jnp.concatenate(halves, axis=0).T
                c0 = (kv * grp + 2 * pair) * d
                o_ref[r0:r0 + BLOCK, c0:c0 + 2 * d] = o_pair.astype(o_ref.dtype)


def swa_attention(qkv, slopes, sinks, batch, seq_len):
    m = qkv.shape[0]
    nt = seq_len // SWA_TILE
    per_tile = SWA_TILE // BLOCK
    hq = SWA_HEADS * SWA_HEAD_DIM
    kv_w = SWA_KV_HEADS * SWA_HEAD_DIM
    k_col = hq // kv_w
    v_col = k_col + 1
    ncol = SWA_HEADS * BLOCK

    def cur(col):
        return lambda b, t: (b * nt + t, col)

    def prev(col):
        return lambda b, t: (jnp.maximum((b * nt + t) * per_tile - 1, 0), col)

    smem = pl.BlockSpec(memory_space=pltpu.SMEM)
    return pl.pallas_call(
        _swa_kernel,
        grid=(batch, nt),
        in_specs=[smem, smem,
                  pl.BlockSpec((SWA_TILE, hq), cur(0)),
                  pl.BlockSpec((SWA_TILE, kv_w), cur(k_col)),
                  pl.BlockSpec((BLOCK, kv_w), prev(k_col)),
                  pl.BlockSpec((SWA_TILE, kv_w), cur(v_col)),
                  pl.BlockSpec((BLOCK, kv_w), prev(v_col))],
        out_specs=pl.BlockSpec((SWA_TILE, hq), cur(0)),
        out_shape=jax.ShapeDtypeStruct((m, hq), jnp.bfloat16),
        scratch_shapes=[pltpu.VMEM((per_tile, 2 * kv_w, ncol), jnp.bfloat16),
                        pltpu.VMEM((per_tile, 2 * BLOCK, 2 * kv_w), jnp.bfloat16),
                        pltpu.VMEM((per_tile, SWA_KV_HEADS, SWA_VROWS, 2 * BLOCK),
                                   jnp.bfloat16),
                        pltpu.VMEM((per_tile, 2 * BLOCK, ncol), jnp.bfloat16)],
        compiler_params=_params(("arbitrary", "arbitrary")),
        name="swa_attention",
    )(slopes, sinks, qkv, qkv, qkv, qkv, qkv)


def _diff_kernel(slopes_ref, lamv_ref, subg_ref, q_ref, k_ref, v_ref, o_ref,
                 kaug, vt, qaug_t, s_tiles, s_pre, *, lambda_init, seq_len):
    hp = pl.program_id(1)
    tq, tk, d = DIFF_TQ, DIFF_TK, DIFF_HEAD_DIM
    hw = 2 * d
    nq = seq_len // tq
    f32, bf16 = jnp.float32, jnp.bfloat16
    heads = range(DIFF_HEADS_PER_STEP)

    def stage_keys_values():
        lane = lax.broadcasted_iota(jnp.int32, (tk, hw), 1)
        row = lax.broadcasted_iota(jnp.int32, (tk, hw), 0)
        for c in range(seq_len // tk):
            r0 = c * tk
            pos = row + r0
            hi = (pos >> 6).astype(f32)
            lo = (pos & 63).astype(f32)
            aux = jnp.where(lane < 3, hi,
                            jnp.where(lane < 6, lo,
                                      jnp.where(lane < 9, 1.0, 0.0)))
            for g in heads:
                cols = slice(g * hw, (g + 1) * hw)
                kaug[g, c, :, 0:hw] = k_ref[r0:r0 + tk, cols]
                kaug[g, c, :, hw:2 * hw] = aux.astype(bf16)
                vt[g, c, 0:hw, :] = (
                    v_ref[r0:r0 + tk, cols].astype(f32).T.astype(bf16))
                vt[g, c, hw:DIFF_VROWS, :] = jnp.ones((DIFF_VROWS - hw, tk), bf16)

    def tile_slot(tile):
        return 2 if tile == 0 else tile % 2

    def setup_queries(g, tile):
        q0 = tile * tq
        q_t = (q_ref[q0:q0 + tq, g * hw:(g + 1) * hw].astype(f32)
               * (LOG2E * d ** -0.5)).T
        qrow = lax.broadcasted_iota(jnp.int32, (hw, tq), 0)
        par = tile_slot(tile)
        qaug_t[g, par, 0:hw, 0:tq] = jnp.where(qrow < d, q_t, 0.0).astype(bf16)
        qaug_t[g, par, 0:hw, tq:2 * tq] = jnp.where(qrow >= d, q_t,
                                                    0.0).astype(bf16)
        a_coef = slopes_ref[hp * DIFF_HEADS_PER_STEP + g] * LOG2E
        c_coef = -a_coef * float(q0)
        r = lax.broadcasted_iota(jnp.int32, (16, 2 * tq), 0)
        t = jnp.where(r < 3, a_coef * 64.0,
                      jnp.where(r < 6, a_coef, jnp.where(r < 9, c_coef, 0.0)))
        t0 = t.astype(bf16).astype(f32)
        t1 = (t - t0).astype(bf16).astype(f32)
        t2 = t - t0 - t1
        first = (r == 0) | (r == 3) | (r == 6)
        second = (r == 1) | (r == 4) | (r == 7)
        qaug_t[g, par, hw:hw + 16, :] = jnp.where(
            first, t0, jnp.where(second, t1, t2)).astype(bf16)
        qaug_t[g, par, hw + 16:2 * hw, :] = jnp.zeros((hw - 16, 2 * tq), bf16)

    half = tk // 2
    late_cols = (slice(half, tq), slice(tq + half, 2 * tq))

    def scores(tile, j, buf, slot, diagonal=False):
        par = tile_slot(tile)
        for g in heads:
            if diagonal:
                buf[g, slot, 0:half, :] = jnp.dot(
                    kaug[g, j, 0:half, :], qaug_t[g, par],
                    preferred_element_type=f32)
                for cols in late_cols:
                    buf[g, slot, half:tk, cols] = jnp.dot(
                        kaug[g, j, half:tk, :], qaug_t[g, par, :, cols],
                        preferred_element_type=f32)
            else:
                s = jnp.dot(kaug[g, j], qaug_t[g, par],
                            preferred_element_type=f32)
                buf[g, slot, 0:tk, :] = s
                buf[g, slot, tk:tk + 1, :] = jnp.max(s, axis=0, keepdims=True)

    def accumulate(m_old, acc, m_new, pv):
        if m_old is None:
            return pv
        return acc * jnp.exp2(m_old - m_new) + pv

    def softmax_pv(j, buf, slot, m_old, acc):
        m_out, acc_out = [], []
        for g in heads:
            s = buf[g, slot, 0:tk, :]
            col_max = buf[g, slot, tk:tk + 1, :]
            m_new = (col_max if m_old[g] is None
                     else jnp.maximum(m_old[g], col_max))
            p = jnp.exp2(s - m_new).astype(bf16)
            pv = jnp.dot(vt[g, j], p, preferred_element_type=f32)
            acc_out.append(accumulate(m_old[g], acc[g], m_new, pv))
            m_out.append(m_new)
        return m_out, acc_out

    def softmax_pv_diagonal(j, buf, slot, m_old, acc):
        m_out, acc_out = [], []
        row = lax.broadcasted_iota(jnp.int32, (half, half), 0)
        col = lax.broadcasted_iota(jnp.int32, (half, half), 1)
        tri = row <= col
        for g in heads:
            top, bot, col_max = [], [], []
            for b in range(2 * tq // half):
                cols = slice(b * half, (b + 1) * half)
                s_top = buf[g, slot, 0:half, cols]
                if b % 2 == 0:
                    s_top = jnp.where(tri, s_top, NEG_BIG)
                    s_bot = None
                    cm = jnp.max(s_top, axis=0, keepdims=True)
                else:
                    s_bot = jnp.where(tri, buf[g, slot, half:tk, cols], NEG_BIG)
                    cm = jnp.maximum(jnp.max(s_top, axis=0, keepdims=True),
                                     jnp.max(s_bot, axis=0, keepdims=True))
                top.append(s_top)
                bot.append(s_bot)
                col_max.append(cm)
            col_max = jnp.concatenate(col_max, axis=1)
            m_new = (col_max if m_old[g] is None
                     else jnp.maximum(m_old[g], col_max))
            p_top = jnp.concatenate(
                [jnp.exp2(s_b - m_new[:, b * half:(b + 1) * half]).astype(bf16)
                 for b, s_b in enumerate(top)], axis=1)
            pv = jnp.dot(vt[g, j, :, 0:half], p_top, preferred_element_type=f32)
            parts = []
            for b, s_b in enumerate(bot):
                cols = slice(b * half, (b + 1) * half)
                if s_b is None:
                    parts.append(pv[:, cols])
                else:
                    p_b = jnp.exp2(s_b - m_new[:, cols]).astype(bf16)
                    parts.append(pv[:, cols] + jnp.dot(
                        vt[g, j, :, half:tk], p_b, preferred_element_type=f32))
            pv = jnp.concatenate(parts, axis=1)
            acc_out.append(accumulate(m_old[g], acc[g], m_new, pv))
            m_out.append(m_new)
        return m_out, acc_out

    def normalise_and_store(qi, acc_all):
        lam = (jnp.exp(jnp.sum(lamv_ref[0:1, :] * lamv_ref[1:2, :]))
               - jnp.exp(jnp.sum(lamv_ref[2:3, :] * lamv_ref[3:4, :]))
               + lambda_init)
        for g in heads:
            acc = acc_all[g]
            o_all = acc[0:hw, :] * (1.0 / acc[hw:hw + 1, :])
            o_t = o_all[:, 0:tq] - lam * o_all[:, tq:2 * tq]
            o_t = o_t * lax.rsqrt(
                jnp.mean(o_t * o_t, axis=0, keepdims=True) + EPS)
            o = o_t.T * (subg_ref[...] * (1.0 - lambda_init))
            o_ref[qi * tq:(qi + 1) * tq,
                  g * hw:(g + 1) * hw] = o.astype(o_ref.dtype)

    def query_tile(c):
        base = (c % 2) * (nq - 1)
        m = [None] * len(heads)
        acc = [None] * len(heads)
        for j in range(c + 1):
            if j < c:
                scores(c, j + 1, s_tiles, base + j, diagonal=(j + 1 == c))
            buf, slot = ((s_pre, tile_slot(c)) if j == 0
                         else (s_tiles, base + j - 1))
            step = softmax_pv_diagonal if j == c else softmax_pv
            m, acc = step(j, buf, slot, m, acc)
        if c + 1 < nq:
            for g in heads:
                setup_queries(g, c + 1)
            scores(c + 1, 0, s_pre, (c + 1) % 2)
        normalise_and_store(c, acc)


    def pair_of_tiles(c):
        if c == 0:
            stage_keys_values()
            for g in heads:
                setup_queries(g, 0)
            scores(0, 0, s_pre, tile_slot(0), diagonal=True)
        query_tile(c)
        query_tile(c + 1)

    def tile_step(i, carry):
        for c in range(0, nq, 2):
            pl.when(i == c // 2)(functools.partial(pair_of_tiles, c))
        return carry

    lax.fori_loop(0, nq // 2, tile_step, 0)


def diff_attention(qkv, slopes, lam_vecs, subln_g, lambda_init, batch, seq_len):
    m = qkv.shape[0]
    hw = 2 * DIFF_HEAD_DIM
    nq = seq_len // DIFF_TQ
    smem = pl.BlockSpec(memory_space=pltpu.SMEM)
    assert DIFF_TQ == DIFF_TK and seq_len % DIFF_TK == 0
    nk = seq_len // DIFF_TK
    hps = DIFF_HEADS_PER_STEP
    groups = DIFF_HEADS // hps
    bw = hps * hw
    score_tile = (DIFF_TK + 8, 2 * DIFF_TQ)
    return pl.pallas_call(
        functools.partial(_diff_kernel, lambda_init=lambda_init,
                          seq_len=seq_len),
        grid=(batch, groups),
        in_specs=[smem,
                  _const_spec(lam_vecs.shape),
                  _const_spec((1, hw)),
                  pl.BlockSpec((seq_len, bw), lambda b, h: (b, h)),
                  pl.BlockSpec((seq_len, bw), lambda b, h: (b, groups + h)),
                  pl.BlockSpec((seq_len, bw), lambda b, h: (b, 2 * groups + h))],
        out_specs=pl.BlockSpec((seq_len, bw), lambda b, h: (b, h)),
        out_shape=jax.ShapeDtypeStruct((m, DIFF_HEADS * hw), jnp.bfloat16),
        scratch_shapes=[pltpu.VMEM((hps, nk, DIFF_TK, 2 * hw), jnp.bfloat16),
                        pltpu.VMEM((hps, nk, DIFF_VROWS, DIFF_TK), jnp.bfloat16),
                        pltpu.VMEM((hps, 3, 2 * hw, 2 * DIFF_TQ), jnp.bfloat16),
                        pltpu.VMEM((hps, 2 * (nk - 1)) + score_tile, jnp.float32),
                        pltpu.VMEM((hps, 3) + score_tile, jnp.float32)],
        compiler_params=_params(("parallel", "parallel")),
        name="diff_attention",
    )(slopes, lam_vecs, subln_g.reshape(1, hw), qkv, qkv, qkv)


def _alibi_slopes(n_heads):
    return np.exp2(-8.0 * (np.arange(n_heads, dtype=np.float32) + 1.0)
                   / n_heads).astype(np.float32)


def _diff_lambda_init(layer_idx):
    return 0.8 - 0.6 * math.exp(-0.3 * layer_idx)


def kernel(x, mix_pre_g, mix_post_g, ffn_pre_g, ffn_post_g, swa_w_qkv, swa_sinks, swa_w_o, diff_w_qkv, diff_lam_q1, diff_lam_k1, diff_lam_q2, diff_lam_k2, diff_subln_g, diff_w_o, ffn_w_up, ffn_conv_w, ffn_conv_b, ffn_w_down):
    batch, seq_len, d = x.shape
    depth = mix_pre_g.shape[0]
    bf = jnp.bfloat16
    swa_w_qkv, swa_w_o = swa_w_qkv.astype(bf), swa_w_o.astype(bf)
    diff_w_qkv, diff_w_o = diff_w_qkv.astype(bf), diff_w_o.astype(bf)
    ffn_w_up, ffn_w_down = ffn_w_up.astype(bf), ffn_w_down.astype(bf)
    x2d = x.reshape(batch * seq_len, d)
    for i in range(depth):
        j = i // N_MIXERS
        if i % N_MIXERS == 0:
            qkv = norm_matmul(x2d, mix_pre_g[i], swa_w_qkv, j, n_chunk=640)
            a = swa_attention(qkv, _alibi_slopes(SWA_HEADS),
                              swa_sinks[j].astype(jnp.float32), batch, seq_len)
            w_o = swa_w_o
        else:
            qkv = norm_matmul(x2d, mix_pre_g[i], diff_w_qkv, j)
            lam_vecs = jnp.stack([diff_lam_q1[j], diff_lam_k1[j],
                                  diff_lam_q2[j], diff_lam_k2[j]]).astype(jnp.float32)
            a = diff_attention(qkv, _alibi_slopes(DIFF_HEADS), lam_vecs,
                               diff_subln_g[j], _diff_lambda_init(i),
                               batch, seq_len)
            w_o = diff_w_o
        x2d, act = mixer_out_ffn_up(a, w_o, j, mix_post_g[i], x2d, ffn_pre_g[i],
                                    ffn_w_up, ffn_conv_w, i, ffn_conv_b[i],
                                    seq_len)
        x2d = matmul_norm_residual(act, ffn_w_down, i, ffn_post_g[i], x2d)
    return x2d.reshape(batch, seq_len, d)
```
